```python
import math
import jax, jax.numpy as jnp
from jax import lax
import numpy as np

D_MODEL = 1024
BATCH = 8
SEQ = 2048
DEPTH = 2
DEC_BATCH = 32
DEC_SEQ = 4
PAST_LEN = 8192
PAGE_SIZE = 128

D_MIX = D_MODEL
D_HGRN = D_MIX // 2
H_HGRN = 4
DK_HGRN = D_HGRN // H_HGRN
DV_HGRN = D_HGRN // H_HGRN
D_FOX = D_MIX - D_HGRN
H_FOX = 8
DH_FOX = D_FOX // H_FOX
D_FF = 2816
CHUNK_HGRN = 64
Q_BLOCK = 128
EPS = 1e-6
TINY = 1e-30
NEG = -1e30
D_IN = 4 * D_HGRN + 3 * D_FOX + H_FOX

kernel_name = 'hymba_hgrn2_fox_macaron_step'


def rmsnorm(x, g):
    xf = x.astype(jnp.float32)
    y = xf * lax.rsqrt(jnp.mean(xf * xf, axis=-1, keepdims=True) + EPS)
    return (y * g.astype(jnp.float32)).astype(x.dtype)


def headnorm(o, gain):
    h, d = o.shape[-2], o.shape[-1]
    y = o * lax.rsqrt(jnp.mean(o * o, axis=-1, keepdims=True) + EPS)
    y = y * gain.astype(jnp.float32).reshape(h, d)
    return y.reshape(o.shape[:-2] + (h * d,))


def swiglu(h, w_in, w_out):
    a, b = jnp.split(h @ w_in, 2, axis=-1)
    return (jax.nn.silu(a) * b) @ w_out


def hgrn_lower_bounds(lb_param):
    p = jax.nn.softmax(lb_param.astype(jnp.float32), axis=0)
    return jnp.cumsum(p, axis=0) - p[0]


def hgrn2(a_q, a_f, a_i, a_g, lb, gain, state0):
    bn, t_len, _ = a_q.shape
    f32 = jnp.float32
    f_gate = lb + (1.0 - lb) * jax.nn.sigmoid(a_f.astype(f32))
    logf = jnp.log(jnp.maximum(f_gate, TINY))
    k = 1.0 - f_gate
    q = a_q.astype(f32)
    v = a_i.astype(f32)
    c = math.gcd(t_len, CHUNK_HGRN)
    n = t_len // c

    def chunks(a):
        return a.reshape(bn, n, c, H_HGRN, a.shape[-1] // H_HGRN).transpose(1, 0, 3, 2, 4)

    qc, kc, lfc, vc = chunks(q), chunks(k), chunks(logf), chunks(v)
    causal = jnp.tril(jnp.ones((c, c), dtype=bool))[:, :, None]

    def step(s_prev, inp):
        qb, kb, lfb, vb = inp
        a_cum = jnp.cumsum(lfb, axis=2)
        diff = a_cum[:, :, :, None, :] - a_cum[:, :, None, :, :]
        decay = jnp.where(causal, jnp.exp(jnp.where(causal, diff, 0.0)), 0.0)
        scores = jnp.einsum('bhtd,bhsd,bhtsd->bhts', qb, kb, decay)
        o = jnp.einsum('bhts,bhsv->bhtv', scores, vb) + jnp.einsum('bhtd,bhdv->bhtv', qb * jnp.exp(a_cum), s_prev)
        a_last = a_cum[:, :, -1:, :]
        s_new = jnp.exp(a_last[:, :, 0, :])[..., None] * s_prev + jnp.einsum('bhsd,bhsv->bhdv', kb * jnp.exp(a_last - a_cum), vb)
        return s_new, o

    s_fin, oc = lax.scan(step, state0.astype(f32), (qc, kc, lfc, vc))
    o = oc.transpose(1, 0, 3, 2, 4).reshape(bn, t_len, H_HGRN, DV_HGRN)
    o = headnorm(o, gain) * jax.nn.silu(a_g.astype(f32))
    return o.astype(a_q.dtype), s_fin.astype(state0.dtype)


def fox_prompt(q, k, v, logf):
    bn, t_len = q.shape[0], q.shape[1]
    f32 = jnp.float32
    scale = DH_FOX ** -0.5
    qf, kf, vf = q.astype(f32) * scale, k.astype(f32), v.astype(f32)
    cum = jnp.cumsum(logf, axis=1).transpose(0, 2, 1)
    nb = t_len // Q_BLOCK
    qb = qf.reshape(bn, nb, Q_BLOCK, H_FOX, DH_FOX).transpose(1, 0, 2, 3, 4)
    cb = cum.reshape(bn, H_FOX, nb, Q_BLOCK).transpose(2, 0, 1, 3)
    pos_q = jnp.arange(t_len).reshape(nb, Q_BLOCK)
    pos_k = jnp.arange(t_len)

    def block(args):
        qi, ci, pq = args
        s = jnp.einsum('bqhd,bkhd->bhqk', qi, kf) + ci[..., None] - cum[:, :, None, :]
        s = jnp.where(pq[:, None] >= pos_k[None, :], s, NEG)
        p = jax.nn.softmax(s, axis=-1)
        return jnp.einsum('bhqk,bkhd->bqhd', p, vf)

    o = lax.map(block, (qb, cb, pos_q))
    return o.transpose(1, 0, 2, 3, 4).reshape(bn, t_len, H_FOX, DH_FOX)


def fox_decode(q, k, v, logf, k_past, v_past, logf_past):
    f32 = jnp.float32
    t_len = q.shape[1]
    p_len = k_past.shape[1]
    scale = DH_FOX ** -0.5
    qf = q.astype(f32) * scale
    lfp = logf_past.astype(f32)
    suffix = (lax.cumsum(lfp, axis=1, reverse=True) - lfp).transpose(0, 2, 1)
    cq = jnp.cumsum(logf, axis=1).transpose(0, 2, 1)
    s_past = jnp.einsum('bqhd,bkhd->bhqk', qf, k_past.astype(f32)) + cq[..., None] + suffix[:, :, None, :]
    s_new = jnp.einsum('bqhd,bkhd->bhqk', qf, k.astype(f32)) + cq[..., :, None] - cq[..., None, :]
    causal = jnp.tril(jnp.ones((t_len, t_len), dtype=bool))
    s_new = jnp.where(causal, s_new, NEG)
    p = jax.nn.softmax(jnp.concatenate([s_past, s_new], axis=-1), axis=-1)
    o = jnp.einsum('bhqk,bkhd->bqhd', p[..., :p_len], v_past.astype(f32))
    return o + jnp.einsum('bhqk,bkhd->bqhd', p[..., p_len:], v.astype(f32))


def token_mix(h, w_in, f_bias, lb, hgrn_gain, fox_gain, w_out, state0, attend):
    bn, t_len, _ = h.shape
    p = h @ w_in
    cuts = [D_HGRN, 2 * D_HGRN, 3 * D_HGRN, 4 * D_HGRN, 4 * D_HGRN + D_FOX, 4 * D_HGRN + 2 * D_FOX, 4 * D_HGRN + 3 * D_FOX]
    a_q, a_f, a_i, a_g, f_q, f_k, f_v, f_f = jnp.split(p, cuts, axis=-1)
    o_a, s_fin = hgrn2(a_q, a_f, a_i, a_g, lb, hgrn_gain, state0)
    q = f_q.reshape(bn, t_len, H_FOX, DH_FOX)
    k = f_k.reshape(bn, t_len, H_FOX, DH_FOX)
    v = f_v.reshape(bn, t_len, H_FOX, DH_FOX)
    logf = jax.nn.log_sigmoid(f_f.astype(jnp.float32) + f_bias.astype(jnp.float32))
    o_f = headnorm(attend(q, k, v, logf), fox_gain).astype(h.dtype)
    out = jnp.concatenate([o_a, o_f], axis=-1) @ w_out
    return out, k, v, logf, s_fin


def setup_inputs(seed: int = 0) -> dict:
    key = jax.random.key(seed)
    ks = jax.random.split(key, 24)
    n_pages = PAST_LEN // PAGE_SIZE
    n_used = DEC_BATCH * n_pages
    n_pool = n_used + max(1, n_used // 4)
    nrm = jax.random.normal
    f32 = jnp.float32
    page_table = jax.random.permutation(ks[0], n_pool)[:n_used].reshape(DEC_BATCH, n_pages).astype(jnp.int32)
    return {
        'x_prompt': nrm(ks[1], (BATCH, SEQ, D_MODEL), f32),
        'x_sample': nrm(ks[2], (DEC_BATCH, DEC_SEQ, D_MODEL), f32),
        'cache_k': nrm(ks[3], (DEPTH, n_pool, PAGE_SIZE, H_FOX, DH_FOX), f32),
        'cache_v': nrm(ks[4], (DEPTH, n_pool, PAGE_SIZE, H_FOX, DH_FOX), f32),
        'cache_logf': jax.nn.log_sigmoid(3.0 + nrm(ks[5], (DEPTH, n_pool, PAGE_SIZE, H_FOX), f32)),
        'state_hgrn': 0.5 * nrm(ks[6], (DEPTH, DEC_BATCH, H_HGRN, DK_HGRN, DV_HGRN), f32),
        'page_table': page_table,
        'norm_ffn1': 1.0 + 0.02 * nrm(ks[7], (DEPTH, D_MODEL), f32),
        'ffn1_w_in': nrm(ks[8], (DEPTH, D_MODEL, 2 * D_FF), f32) * D_MODEL ** -0.5,
        'ffn1_w_out': nrm(ks[9], (DEPTH, D_FF, D_MODEL), f32) * D_FF ** -0.5,
        'norm_mix': 1.0 + 0.02 * nrm(ks[10], (DEPTH, D_MODEL), f32),
        'w_in_mix': nrm(ks[11], (DEPTH, D_MODEL, D_IN), f32) * D_MODEL ** -0.5,
        'hgrn_lb': nrm(ks[12], (DEPTH, D_HGRN), f32),
        'fox_f_bias': 3.0 + 0.5 * nrm(ks[13], (DEPTH, H_FOX), f32),
        'hgrn_out_gain': 1.0 + 0.02 * nrm(ks[14], (DEPTH, D_HGRN), f32),
        'fox_out_gain': 1.0 + 0.02 * nrm(ks[15], (DEPTH, D_FOX), f32),
        'w_out_mix': nrm(ks[16], (DEPTH, D_MIX, D_MODEL), f32) * D_MIX ** -0.5,
        'norm_ffn2': 1.0 + 0.02 * nrm(ks[17], (DEPTH, D_MODEL), f32),
        'ffn2_w_in': nrm(ks[18], (DEPTH, D_MODEL, 2 * D_FF), f32) * D_MODEL ** -0.5,
        'ffn2_w_out': nrm(ks[19], (DEPTH, D_FF, D_MODEL), f32) * D_FF ** -0.5,
        'norm_final': 1.0 + 0.02 * nrm(ks[20], (D_MODEL,), f32),
    }


def reference(x_prompt, x_sample, cache_k, cache_v, cache_logf, state_hgrn, page_table,
              norm_ffn1, ffn1_w_in, ffn1_w_out, norm_mix, w_in_mix, hgrn_lb, fox_f_bias,
              hgrn_out_gain, fox_out_gain, w_out_mix, norm_ffn2, ffn2_w_in, ffn2_w_out, norm_final):
    lbs = hgrn_lower_bounds(hgrn_lb)
    db = x_sample.shape[0]

    def sample_attend(l):
        k_past = cache_k[l, page_table].reshape(db, -1, H_FOX, DH_FOX)
        v_past = cache_v[l, page_table].reshape(db, -1, H_FOX, DH_FOX)
        lf_past = cache_logf[l, page_table].reshape(db, -1, H_FOX)
        return lambda q, k, v, lf: fox_decode(q, k, v, lf, k_past, v_past, lf_past)

    def run(x, state0_of, attend_of):
        ks, vs, lfs, ss = [], [], [], []
        for l in range(DEPTH):
            x = x + 0.5 * swiglu(rmsnorm(x, norm_ffn1[l]), ffn1_w_in[l], ffn1_w_out[l])
            m, k, v, lf, s_fin = token_mix(rmsnorm(x, norm_mix[l]), w_in_mix[l], fox_f_bias[l], lbs[l],
                                           hgrn_out_gain[l], fox_out_gain[l], w_out_mix[l],
                                           state0_of(l), attend_of(l))
            x = x + m
            x = x + 0.5 * swiglu(rmsnorm(x, norm_ffn2[l]), ffn2_w_in[l], ffn2_w_out[l])
            ks.append(k)
            vs.append(v)
            lfs.append(lf)
            ss.append(s_fin)
        return rmsnorm(x, norm_final), jnp.stack(ks), jnp.stack(vs), jnp.stack(lfs), jnp.stack(ss)

    zero_state = jnp.zeros((x_prompt.shape[0], H_HGRN, DK_HGRN, DV_HGRN), state_hgrn.dtype)
    y_prompt, new_k_prompt, new_v_prompt, new_logf_prompt, new_hgrn_prompt = run(
        x_prompt, lambda l: zero_state, lambda l: fox_prompt)
    y_sample, new_k_sample, new_v_sample, new_logf_sample, new_hgrn_sample = run(
        x_sample, lambda l: state_hgrn[l], sample_attend)
    return (y_prompt, y_sample, new_k_prompt, new_v_prompt, new_logf_prompt, new_hgrn_prompt,
            new_k_sample, new_v_sample, new_logf_sample, new_hgrn_sample)
```

```python
import functools

import jax
import jax.numpy as jnp
from jax import lax
from jax.experimental import pallas as pl
from jax.experimental.pallas import tpu as pltpu

F32 = jnp.float32
BF16 = jnp.bfloat16

EPS = 1e-6
TINY = 1e-30
NEG = -1e30

H_HGRN = 4
H_FOX = 8
LANES = 128
SUBLANES = 8
VMEM_LIMIT = 56 * 1024 * 1024

NT_DIMS = (((1,), (1,)), ((), ()))
TN_DIMS = (((0,), (0,)), ((), ()))


def _params(*sem):
    return pltpu.CompilerParams(dimension_semantics=sem, vmem_limit_bytes=VMEM_LIMIT)


def _resident(shape, index_map):
    return pl.BlockSpec(shape, index_map, pipeline_mode=pl.Buffered(1))


def _rms(x, g):
    return x * lax.rsqrt(jnp.mean(x * x, axis=-1, keepdims=True) + EPS) * g


def _split3(x):
    hi = x.astype(BF16)
    r = x - hi.astype(F32)
    mid = r.astype(BF16)
    lo = (r - mid.astype(F32)).astype(BF16)
    return hi, mid, lo


def _dot01_left(m01, x):
    return sum(jnp.dot(m01, t, preferred_element_type=F32) for t in _split3(x))


def _dot01_right(x, m01):
    return sum(jnp.dot(t, m01, preferred_element_type=F32) for t in _split3(x))


def _ffn_kernel(x_ref, g_ref, wa_ref, wb_ref, wo_ref, *rest, final_norm):
    if final_norm:
        gf_ref, o_ref = rest
    else:
        (o_ref,) = rest
    x = x_ref[...]
    xn = _rms(x, g_ref[...]).astype(BF16)
    a = jnp.dot(xn, wa_ref[...], preferred_element_type=F32)
    b = jnp.dot(xn, wb_ref[...], preferred_element_type=F32)
    h = (a * jax.nn.sigmoid(a) * b).astype(BF16)
    y = x + 0.5 * jnp.dot(h, wo_ref[...], preferred_element_type=F32)
    if final_norm:
        y = _rms(y, gf_ref[...])
    o_ref[...] = y


def _ffn(x, g, w_in, w_out, final_g=None, tm=512):
    n, d = x.shape
    ff = w_out.shape[0]
    tm = min(tm, n)
    in_specs = [
        pl.BlockSpec((tm, d), lambda i: (i, 0)),
        _resident((1, d), lambda i: (0, 0)),
        _resident((d, ff), lambda i: (0, 0)),
        _resident((d, ff), lambda i: (0, 1)),
        _resident((ff, d), lambda i: (0, 0)),
    ]
    args = [x, g.reshape(1, d), w_in, w_in, w_out]
    if final_g is not None:
        in_specs.append(_resident((1, d), lambda i: (0, 0)))
        args.append(final_g.reshape(1, d))
    return pl.pallas_call(
        functools.partial(_ffn_kernel, final_norm=final_g is not None),
        grid=(n // tm,),
        in_specs=in_specs,
        out_specs=pl.BlockSpec((tm, d), lambda i: (i, 0)),
        out_shape=jax.ShapeDtypeStruct((n, d), F32),
        compiler_params=_params("parallel"),
        name="ffn",
    )(*args)


def _inproj_kernel(x_ref, g_ref, w_ref, wf_ref, fb_ref,
                   hg_ref, q_ref, k_ref, v_ref, lf_ref, cum_ref, cumt_ref,
                   carry_ref, *, seq_len, d_hg, d_fox):
    tm = x_ref.shape[0]
    i = pl.program_id(0)
    xn = _rms(x_ref[...], g_ref[...]).astype(BF16)
    p = jnp.dot(xn, w_ref[...], preferred_element_type=F32)
    hg_ref[...] = p[:, :d_hg]
    q_ref[...] = p[:, d_hg:d_hg + d_fox]
    k_ref[...] = p[:, d_hg + d_fox:d_hg + 2 * d_fox]
    v_ref[...] = p[:, d_hg + 2 * d_fox:d_hg + 3 * d_fox]

    z = jnp.dot(xn, wf_ref[...], preferred_element_type=F32) + fb_ref[...]
    lf = jnp.minimum(z, 0.0) - jnp.log1p(jnp.exp(-jnp.abs(z)))

    row = lax.broadcasted_iota(jnp.int32, (tm, tm), 0)
    col = lax.broadcasted_iota(jnp.int32, (tm, tm), 1)
    same = col <= row
    if seq_len < tm:
        same = same & ((row // seq_len) == (col // seq_len))
    cum = _dot01_left(same.astype(BF16), lf)
    if seq_len > tm:
        tiles_per_seq = seq_len // tm

        @pl.when(i % tiles_per_seq == 0)
        def _():
            carry_ref[...] = jnp.zeros_like(carry_ref)

        cum = cum + carry_ref[...]
        carry_ref[...] = cum[tm - 1:tm, :]
    lf_ref[...] = lf[:, :H_FOX]
    cum_ref[...] = cum[:, :H_FOX]
    cumt_ref[0] = cum.T[:H_FOX, :]


def _inproj(x, g, w_main, w_f, f_bias, seq_len, tm=512):
    n, d = x.shape
    tm = min(tm, n)
    assert seq_len % tm == 0 or tm % seq_len == 0
    d_main = w_main.shape[1]
    d_fox = 512
    d_hg = d_main - 3 * d_fox
    nt = n // tm
    tok = lambda w: pl.BlockSpec((tm, w), lambda i: (i, 0))
    return pl.pallas_call(
        functools.partial(_inproj_kernel, seq_len=seq_len, d_hg=d_hg, d_fox=d_fox),
        grid=(nt,),
        in_specs=[
            tok(d),
            _resident((1, d), lambda i: (0, 0)),
            _resident((d, d_main), lambda i: (0, 0)),
            _resident((d, LANES), lambda i: (0, 0)),
            _resident((1, LANES), lambda i: (0, 0)),
        ],
        out_specs=[tok(d_hg), tok(d_fox), tok(d_fox), tok(d_fox), tok(H_FOX), tok(H_FOX),
                   pl.BlockSpec((1, H_FOX, tm), lambda i: (i, 0, 0))],
        out_shape=[
            jax.ShapeDtypeStruct((n, d_hg), F32),
            jax.ShapeDtypeStruct((n, d_fox), F32),
            jax.ShapeDtypeStruct((n, d_fox), F32),
            jax.ShapeDtypeStruct((n, d_fox), F32),
            jax.ShapeDtypeStruct((n, H_FOX), F32),
            jax.ShapeDtypeStruct((n, H_FOX), F32),
            jax.ShapeDtypeStruct((nt, H_FOX, tm), F32),
        ],
        scratch_shapes=[pltpu.VMEM((1, LANES), F32)],
        compiler_params=_params("arbitrary"),
        name="inproj",
    )(x, g.reshape(1, d), w_main, w_f, f_bias)


def _hgrn_kernel(lbp_ref, aq_ref, af_ref, ai_ref, ag_ref, gain_ref, *rest,
                 layer, chunk, sub, valid_len, zero_init):
    if zero_init:
        o_ref, sout_ref, st_ref = rest
    else:
        s0_ref, o_ref, sout_ref, st_ref = rest
    tb = aq_ref.shape[1]
    nchunks = tb // chunk
    t = pl.program_id(2)

    @pl.when(t == 0)
    def _():
        if zero_init:
            st_ref[...] = jnp.zeros_like(st_ref)
        else:
            st_ref[...] = s0_ref[0, 0, 0].T

    lbp = lbp_ref[...]
    e = jnp.exp(lbp - jnp.max(lbp, axis=0, keepdims=True))
    prob = e / jnp.sum(e, axis=0, keepdims=True)
    lb = jnp.zeros((1, LANES), F32)
    for j in range(1, layer + 1):
        lb = lb + prob[j:j + 1, :]

    row = lax.broadcasted_iota(jnp.int32, (chunk, 1), 0)
    rmod = row % sub
    tri = (lax.broadcasted_iota(jnp.int32, (chunk, chunk), 1)
           <= lax.broadcasted_iota(jnp.int32, (chunk, chunk), 0)).astype(BF16)
    gain = gain_ref[...]

    def chunk_body(c, carry):
        r0 = pl.multiple_of(c * chunk, chunk)
        q = aq_ref[0, pl.ds(r0, chunk), :]
        f = lb + (1.0 - lb) * jax.nn.sigmoid(af_ref[0, pl.ds(r0, chunk), :])
        v = ai_ref[0, pl.ds(r0, chunk), :]
        g = ag_ref[0, pl.ds(r0, chunk), :]
        logf = jnp.log(jnp.maximum(f, TINY))
        k = 1.0 - f
        if valid_len is not None:
            live = (t * tb + r0 + row) < valid_len
            logf = jnp.where(live, logf, 0.0)
            k = jnp.where(live, k, 0.0)
        a = _dot01_left(tri, logf)
        st = st_ref[...]
        v_bf = v.astype(BF16)

        o = lax.dot_general((q * jnp.exp(a)).astype(BF16), st.astype(BF16), NT_DIMS,
                            preferred_element_type=F32)

        for d in range(sub):
            if d == 0:
                kd, ad, vd = k, a, v
            else:
                kd = pltpu.roll(k, d, 0)
                ad = pltpu.roll(a, d, 0)
                vd = pltpu.roll(v, d, 0)
            ok = rmod >= d
            w = q * kd * jnp.exp(jnp.where(ok, a - ad, 0.0))
            sc = jnp.where(ok, jnp.sum(w, axis=-1, keepdims=True), 0.0)
            o = o + sc * vd

        pieces = [jnp.zeros((sub, LANES), F32)]
        for i in range(1, chunk // sub):
            lo, hi = i * sub, (i + 1) * sub
            r = a[lo - 1:lo, :]
            qt = (q[lo:hi] * jnp.exp(a[lo:hi] - r)).astype(BF16)
            kt = (k[:lo] * jnp.exp(r - a[:lo])).astype(BF16)
            sc = lax.dot_general(qt, kt, NT_DIMS, preferred_element_type=F32)
            pieces.append(jnp.dot(sc.astype(BF16), v_bf[:lo], preferred_element_type=F32))
        if len(pieces) > 1:
            o = o + jnp.concatenate(pieces, axis=0)

        a_last = a[chunk - 1:chunk, :]
        kt = (k * jnp.exp(a_last - a)).astype(BF16)
        st_ref[...] = st * jnp.exp(a_last) + lax.dot_general(
            v_bf, kt, TN_DIMS, preferred_element_type=F32)

        y = o * lax.rsqrt(jnp.mean(o * o, axis=-1, keepdims=True) + EPS) * gain
        y = y * (g * jax.nn.sigmoid(g))
        o_ref[0, pl.ds(r0, chunk), :] = y.astype(o_ref.dtype)
        return carry

    lax.fori_loop(0, nchunks, chunk_body, 0)

    @pl.when(t == pl.num_programs(2) - 1)
    def _():
        sout_ref[0, 0] = st_ref[...].T


def _hgrn(hg, lb_param, gain, state0, layer, *, tb, chunk, sub, valid_len, out_dtype):
    bn, t_len, d4 = hg.shape
    dh = d4 // 4
    nh = dh // LANES
    depth = lb_param.shape[0]
    blk = lambda c: pl.BlockSpec((1, tb, LANES), lambda b, h, t: (b, t, h + c * nh))
    in_specs = [pl.BlockSpec((depth, LANES), lambda b, h, t: (0, h)),
                blk(0), blk(1), blk(2), blk(3),
                pl.BlockSpec((1, LANES), lambda b, h, t: (0, h))]
    args = [lb_param, hg, hg, hg, hg, gain.reshape(1, dh)]
    if state0 is not None:
        in_specs.append(pl.BlockSpec((1, 1, 1, LANES, LANES), lambda b, h, t: (layer, b, h, 0, 0)))
        args.append(state0)
    return pl.pallas_call(
        functools.partial(_hgrn_kernel, layer=layer, chunk=chunk, sub=sub,
                          valid_len=valid_len, zero_init=state0 is None),
        grid=(bn, nh, t_len // tb),
        in_specs=in_specs,
        out_specs=[pl.BlockSpec((1, tb, LANES), lambda b, h, t: (b, t, h)),
                   pl.BlockSpec((1, 1, LANES, LANES), lambda b, h, t: (b, h, 0, 0))],
        out_shape=[jax.ShapeDtypeStruct((bn, t_len, dh), out_dtype),
                   jax.ShapeDtypeStruct((bn, nh, LANES, LANES), F32)],
        scratch_shapes=[pltpu.VMEM((LANES, LANES), F32)],
        compiler_params=_params("parallel", "parallel", "arbitrary"),
        name="hgrn",
    )(*args)


def _fox_prompt_kernel(q_ref, k_ref, v_ref, cum_ref, cumt_ref, gain_ref, o_ref,
                       m_ref, l_ref, acc_ref, *, scale):
    tq = q_ref.shape[1]
    dh = LANES // 2
    hp = pl.program_id(1)
    qi = pl.program_id(2)
    lane = lax.broadcasted_iota(jnp.int32, (1, LANES), 1)
    q = q_ref[0] * scale
    cum = cum_ref[0]
    hl = lax.broadcasted_iota(jnp.int32, (1, H_FOX), 1)
    qs, cqs = [], []
    for hh in range(2):
        in_head = (lane // dh) == hh
        qs.append(jnp.where(in_head, q, 0.0).astype(BF16))
        cqs.append(jnp.sum(jnp.where(hl == 2 * hp + hh, cum, 0.0), axis=-1, keepdims=True))
        m_ref[hh] = jnp.full((tq, LANES), NEG, F32)
        l_ref[hh] = jnp.zeros((tq, LANES), F32)
        acc_ref[hh] = jnp.zeros((tq, LANES), F32)

    def block(kb, masked):
        k0 = pl.multiple_of(kb * tq, tq)
        k = k_ref[0, pl.ds(k0, tq), :].astype(BF16)
        v = v_ref[0, pl.ds(k0, tq), :].astype(BF16)
        for hh in range(2):
            ck = cumt_ref[0, pl.ds(2 * hp + hh, 1), pl.ds(k0, tq)]
            s = lax.dot_general(qs[hh], k, NT_DIMS, preferred_element_type=F32)
            s = s + cqs[hh] - ck
            if masked:
                r = lax.broadcasted_iota(jnp.int32, (tq, tq), 0)
                c = lax.broadcasted_iota(jnp.int32, (tq, tq), 1)
                s = jnp.where(r >= c, s, NEG)
            m_prev = m_ref[hh]
            m_new = jnp.maximum(m_prev, jnp.max(s, axis=-1, keepdims=True))
            alpha = jnp.exp(m_prev - m_new)
            p = jnp.exp(s - m_new[:, :1])
            l_ref[hh] = alpha * l_ref[hh] + jnp.sum(p, axis=-1, keepdims=True)
            acc_ref[hh] = alpha * acc_ref[hh] + jnp.dot(p.astype(BF16), v,
                                                         preferred_element_type=F32)
            m_ref[hh] = m_new

    def body(kb, carry):
        block(kb, masked=False)
        return carry

    lax.fori_loop(0, qi, body, 0)
    block(qi, masked=True)

    o0 = acc_ref[0] / l_ref[0]
    o1 = acc_ref[1] / l_ref[1]
    low = (lane // dh) == 0
    o = jnp.where(low, o0, o1)
    sq = o * o
    ss0 = jnp.sum(jnp.where(low, sq, 0.0), axis=-1, keepdims=True)
    ss1 = jnp.sum(jnp.where(low, 0.0, sq), axis=-1, keepdims=True)
    inv = jnp.where(low, lax.rsqrt(ss0 / dh + EPS), lax.rsqrt(ss1 / dh + EPS))
    o_ref[0] = (o * inv * gain_ref[...]).astype(o_ref.dtype)


def _fox_prompt(q, k, v, cum, cumt, gain, tq=512):
    bn, t_len, d = q.shape
    tq = min(tq, t_len)
    nhp = d // LANES
    scale = float((LANES // 2) ** -0.5)
    return pl.pallas_call(
        functools.partial(_fox_prompt_kernel, scale=scale),
        grid=(bn, nhp, t_len // tq),
        in_specs=[
            pl.BlockSpec((1, tq, LANES), lambda b, h, i: (b, i, h)),
            pl.BlockSpec((1, t_len, LANES), lambda b, h, i: (b, 0, h)),
            pl.BlockSpec((1, t_len, LANES), lambda b, h, i: (b, 0, h)),
            pl.BlockSpec((1, tq, H_FOX), lambda b, h, i: (b, i, 0)),
            pl.BlockSpec((1, H_FOX, t_len), lambda b, h, i: (b, 0, 0)),
            pl.BlockSpec((1, LANES), lambda b, h, i: (0, h)),
        ],
        out_specs=pl.BlockSpec((1, tq, LANES), lambda b, h, i: (b, i, h)),
        out_shape=jax.ShapeDtypeStruct((bn, t_len, d), BF16),
        scratch_shapes=[pltpu.VMEM((2, tq, LANES), F32),
                        pltpu.VMEM((2, tq, LANES), F32),
                        pltpu.VMEM((2, tq, LANES), F32)],
        compiler_params=_params("parallel", "parallel", "arbitrary"),
        name="fox_prompt",
    )(q, k, v, cum, cumt, gain.reshape(1, d))


def _suffix_kernel(lf_ref, suf_ref, tot_ref):
    n = lf_ref.shape[1]
    i = lax.broadcasted_iota(jnp.int32, (n, n), 0)
    j = lax.broadcasted_iota(jnp.int32, (n, n), 1)
    lf = lf_ref[...]
    suf_ref[...] = _dot01_right(lf, (i > j).astype(BF16))
    tot_ref[...] = _dot01_right(lf, jnp.ones((n, n), BF16))


def _suffix(lf_t, tr=1024):
    rows, n = lf_t.shape
    tr = min(tr, rows)
    assert rows % tr == 0
    spec = pl.BlockSpec((tr, n), lambda i: (i, 0))
    return pl.pallas_call(
        _suffix_kernel,
        grid=(rows // tr,),
        in_specs=[spec],
        out_specs=[spec, spec],
        out_shape=[jax.ShapeDtypeStruct((rows, n), F32)] * 2,
        compiler_params=_params("parallel"),
        name="suffix",
    )(lf_t)


def _fox_decode_kernel(pt_ref, q_ref, kn_ref, vn_ref, cq_ref, gain_ref, *rest, pps, scale):
    k_refs = rest[:pps]
    v_refs = rest[pps:2 * pps]
    suf_refs = rest[2 * pps:3 * pps]
    tot_refs = rest[3 * pps:4 * pps]
    o_ref, m_ref, l_ref, acc_ref, carry_ref, qbd_ref = rest[4 * pps:]
    t_new = q_ref.shape[1]
    d = q_ref.shape[2]
    dh = d // H_FOX
    rows = t_new * H_FOX
    page = k_refs[0].shape[1]
    j = pl.program_id(1)
    head_of_lane = lax.broadcasted_iota(jnp.int32, (H_FOX, d), 1) // dh
    head_of_row = lax.broadcasted_iota(jnp.int32, (H_FOX, d), 0)
    diag = head_of_lane == head_of_row
    cq = cq_ref[0]

    @pl.when(j == 0)
    def _():
        qbd = jnp.concatenate(
            [jnp.where(diag, q_ref[0, t:t + 1, :] * scale, 0.0) for t in range(t_new)], axis=0)
        qbd_ref[...] = qbd
        qrow = lax.broadcasted_iota(jnp.int32, (rows, 1), 0) // H_FOX
        s_new = []
        for s in range(t_new):
            dot_s = jnp.sum(qbd * kn_ref[0, s:s + 1, :], axis=-1, keepdims=True)
            cs = jnp.concatenate([cq[s * H_FOX:(s + 1) * H_FOX]] * t_new, axis=0)
            s_new.append(jnp.where(qrow >= s, dot_s + cq - cs, NEG))
        m0 = s_new[0]
        for s in range(1, t_new):
            m0 = jnp.maximum(m0, s_new[s])
        l0 = jnp.zeros((rows, 1), F32)
        acc0 = jnp.zeros((rows, d), F32)
        for s in range(t_new):
            p_s = jnp.exp(s_new[s] - m0)
            l0 = l0 + p_s
            acc0 = acc0 + p_s * vn_ref[0, s:s + 1, :]
        m_ref[...] = jnp.broadcast_to(m0, (rows, LANES))
        l_ref[...] = jnp.broadcast_to(l0, (rows, LANES))
        acc_ref[...] = acc0
        carry_ref[...] = jnp.zeros_like(carry_ref)

    qbd = qbd_ref[...].astype(BF16)
    carry = carry_ref[...]
    s_parts = []
    for i in range(pps):
        kpg = k_refs[i][0].astype(BF16)
        s = lax.dot_general(qbd, kpg, NT_DIMS, preferred_element_type=F32)
        suf = suf_refs[i][0] + carry
        carry = carry + tot_refs[i][0]
        s_parts.append(s + jnp.concatenate([suf] * t_new, axis=0) + cq)
    carry_ref[...] = carry
    s_all = jnp.concatenate(s_parts, axis=1)
    m_prev = m_ref[...]
    m_new = jnp.maximum(m_prev, jnp.max(s_all, axis=-1, keepdims=True))
    alpha = jnp.exp(m_prev - m_new)
    p = jnp.exp(s_all - m_new[:, :1])
    l_ref[...] = alpha * l_ref[...] + jnp.sum(p, axis=-1, keepdims=True)
    m_ref[...] = m_new
    p_bf = p.astype(BF16)
    pv = jnp.zeros((rows, d), F32)
    for i in range(pps):
        pv = pv + jnp.dot(p_bf[:, i * page:(i + 1) * page], v_refs[i][0].astype(BF16),
                          preferred_element_type=F32)
    acc_ref[...] = alpha[:, :1] * acc_ref[...] + pv

    @pl.when(j == pl.num_programs(1) - 1)
    def _():
        o = acc_ref[...] / l_ref[...][:, :1]
        o = jnp.where(jnp.concatenate([diag] * t_new, axis=0), o, 0.0)
        y = o * lax.rsqrt(jnp.sum(o * o, axis=-1, keepdims=True) / dh + EPS)
        out = jnp.concatenate(
            [jnp.sum(y[t * H_FOX:(t + 1) * H_FOX], axis=0, keepdims=True) for t in range(t_new)],
            axis=0)
        o_ref[0] = out * gain_ref[...]


def _fox_decode(q, k_new, v_new, cq_col, gain, cache_k, cache_v, suf, tot, page_table,
                layer, n_pool, pps=8):
    bn, t_new, d = q.shape
    page = cache_k.shape[1]
    n_pages = page_table.shape[1]
    pps = min(pps, n_pages)
    assert n_pages % pps == 0
    rows = t_new * H_FOX
    scale = float((d // H_FOX) ** -0.5)
    base = layer * n_pool

    def page_idx(i):
        return lambda b, j, pt: (base + pt[b, n_pages - 1 - (j * pps + i)], 0, 0)

    tok = pl.BlockSpec((1, t_new, d), lambda b, j, pt: (b, 0, 0))
    in_specs = [tok, tok, tok,
                pl.BlockSpec((1, rows, 1), lambda b, j, pt: (b, 0, 0)),
                pl.BlockSpec((1, d), lambda b, j, pt: (0, 0))]
    in_specs += [pl.BlockSpec((1, page, d), page_idx(i)) for i in range(pps)]
    in_specs += [pl.BlockSpec((1, page, d), page_idx(i)) for i in range(pps)]
    in_specs += [pl.BlockSpec((1, H_FOX, page), page_idx(i)) for i in range(pps)]
    in_specs += [pl.BlockSpec((1, H_FOX, page), page_idx(i)) for i in range(pps)]
    grid_spec = pltpu.PrefetchScalarGridSpec(
        num_scalar_prefetch=1,
        grid=(bn, n_pages // pps),
        in_specs=in_specs,
        out_specs=pl.BlockSpec((1, t_new, d), lambda b, j, pt: (b, 0, 0)),
        scratch_shapes=[pltpu.VMEM((rows, LANES), F32),
                        pltpu.VMEM((rows, LANES), F32),
                        pltpu.VMEM((rows, d), F32),
                        pltpu.VMEM((H_FOX, page), F32),
                        pltpu.VMEM((rows, d), F32)],
    )
    return pl.pallas_call(
        functools.partial(_fox_decode_kernel, pps=pps, scale=scale),
        grid_spec=grid_spec,
        out_shape=jax.ShapeDtypeStruct((bn, t_new, d), F32),
        compiler_params=_params("parallel", "arbitrary"),
        name="fox_decode",
    )(page_table, q, k_new, v_new, cq_col, gain.reshape(1, d),
      *([cache_k] * pps), *([cache_v] * pps), *([suf] * pps), *([tot] * pps))


def _outproj_kernel(x_ref, oa_ref, of_ref, wa_ref, wf_ref, o_ref):
    o_ref[...] = (x_ref[...]
                  + jnp.dot(oa_ref[...], wa_ref[...], preferred_element_type=F32)
                  + jnp.dot(of_ref[...], wf_ref[...], preferred_element_type=F32))


def _outproj(x, o_a, o_f, w_out, tm=512):
    n, d = x.shape
    da, df = o_a.shape[1], o_f.shape[1]
    tm = min(tm, n)
    return pl.pallas_call(
        _outproj_kernel,
        grid=(n // tm,),
        in_specs=[
            pl.BlockSpec((tm, d), lambda i: (i, 0)),
            pl.BlockSpec((tm, da), lambda i: (i, 0)),
            pl.BlockSpec((tm, df), lambda i: (i, 0)),
            _resident((da, d), lambda i: (0, 0)),
            _resident((df, d), lambda i: (1, 0)),
        ],
        out_specs=pl.BlockSpec((tm, d), lambda i: (i, 0)),
        out_shape=jax.ShapeDtypeStruct((n, d), F32),
        compiler_params=_params("parallel"),
        name="outproj",
    )(x, o_a, o_f, w_out, w_out)


def kernel(x_prompt, x_sample, cache_k, cache_v, cache_logf, state_hgrn, page_table, norm_ffn1, ffn1_w_in, ffn1_w_out, norm_mix, w_in_mix, hgrn_lb, fox_f_bias, hgrn_out_gain, fox_out_gain, w_out_mix, norm_ffn2, ffn2_w_in, ffn2_w_out, norm_final):
    depth = norm_ffn1.shape[0]
    bn, t_len, d = x_prompt.shape
    db, t_new, _ = x_sample.shape
    d_fox = fox_out_gain.shape[1]
    d_h = hgrn_out_gain.shape[1]
    dh_fox = d_fox // H_FOX
    n_pool, page = cache_k.shape[1], cache_k.shape[2]
    d_main = 4 * d_h + 3 * d_fox

    w1_in, w1_out = ffn1_w_in.astype(BF16), ffn1_w_out.astype(BF16)
    w2_in, w2_out = ffn2_w_in.astype(BF16), ffn2_w_out.astype(BF16)
    w_main = w_in_mix[:, :, :d_main].astype(BF16)
    w_f = jnp.pad(w_in_mix[:, :, d_main:], ((0, 0), (0, 0), (0, LANES - H_FOX))).astype(BF16)
    f_bias = jnp.pad(fox_f_bias, ((0, 0), (0, LANES - H_FOX))).reshape(depth, 1, LANES)
    w_out = w_out_mix.astype(BF16)

    ck = cache_k.reshape(depth * n_pool, page, d_fox)
    cv = cache_v.reshape(depth * n_pool, page, d_fox)
    lf_t = jnp.swapaxes(cache_logf, 2, 3).reshape(depth * n_pool * H_FOX, page)
    suf, tot = _suffix(lf_t)
    suf = suf.reshape(depth * n_pool, H_FOX, page)
    tot = tot.reshape(depth * n_pool, H_FOX, page)

    t_pad = -(-t_new // SUBLANES) * SUBLANES

    def run(x, seq_len, nseq, prompt):
        ks, vs, lfs, ss = [], [], [], []
        for l in range(depth):
            x = _ffn(x, norm_ffn1[l], w1_in[l], w1_out[l])
            hg, q, k, v, lf, cum, cumt = _inproj(x, norm_mix[l], w_main[l], w_f[l], f_bias[l], seq_len)
            if prompt:
                o_a, s_fin = _hgrn(hg.reshape(nseq, seq_len, 4 * d_h), hgrn_lb, hgrn_out_gain[l], None, l,
                                   tb=min(512, seq_len), chunk=64, sub=16, valid_len=None, out_dtype=BF16)
                o_a = o_a.reshape(nseq * seq_len, d_h)
                tiles = cumt.shape[0] // nseq
                cumt_b = cumt.reshape(nseq, tiles, H_FOX, -1).transpose(0, 2, 1, 3).reshape(nseq, H_FOX, seq_len)
                o_f = _fox_prompt(q.reshape(nseq, seq_len, d_fox), k.reshape(nseq, seq_len, d_fox),
                                  v.reshape(nseq, seq_len, d_fox), cum.reshape(nseq, seq_len, H_FOX),
                                  cumt_b, fox_out_gain[l]).reshape(nseq * seq_len, d_fox)
            else:
                hg_p = jnp.pad(hg.reshape(nseq, seq_len, 4 * d_h), ((0, 0), (0, t_pad - seq_len), (0, 0)))
                o_a, s_fin = _hgrn(hg_p, hgrn_lb, hgrn_out_gain[l], state_hgrn, l,
                                   tb=t_pad, chunk=t_pad, sub=t_pad, valid_len=seq_len, out_dtype=F32)
                o_a = o_a[:, :seq_len].reshape(nseq * seq_len, d_h).astype(BF16)
                o_f = _fox_decode(q.reshape(nseq, seq_len, d_fox), k.reshape(nseq, seq_len, d_fox),
                                  v.reshape(nseq, seq_len, d_fox), cum.reshape(nseq, seq_len * H_FOX, 1),
                                  fox_out_gain[l], ck, cv, suf, tot, page_table, l, n_pool)
                o_f = o_f.reshape(nseq * seq_len, d_fox).astype(BF16)
            x = _outproj(x, o_a, o_f, w_out[l])
            x = _ffn(x, norm_ffn2[l], w2_in[l], w2_out[l],
                     final_g=norm_final if l == depth - 1 else None)
            ks.append(k.reshape(nseq, seq_len, H_FOX, dh_fox))
            vs.append(v.reshape(nseq, seq_len, H_FOX, dh_fox))
            lfs.append(lf.reshape(nseq, seq_len, H_FOX))
            ss.append(s_fin)
        return (x.reshape(nseq, seq_len, d), jnp.stack(ks), jnp.stack(vs), jnp.stack(lfs), jnp.stack(ss))

    y_p, k_p, v_p, lf_p, s_p = run(x_prompt.reshape(bn * t_len, d), t_len, bn, True)
    y_s, k_s, v_s, lf_s, s_s = run(x_sample.reshape(db * t_new, d), t_new, db, False)
    return (y_p, y_s, k_p, v_p, lf_p, s_p, k_s, v_s, lf_s, s_s)
```

```python
import functools

import jax
import jax.numpy as jnp
from jax import lax
from jax.experimental import pallas as pl
from jax.experimental.pallas import tpu as pltpu

F32 = jnp.float32
BF16 = jnp.bfloat16

EPS = 1e-6
TINY = 1e-30
NEG = -1e30

LOG2E = 1.4426950408889634
H_HGRN = 4
H_FOX = 8
FOX_ROW_CHUNK = 64
LANES = 128
SUBLANES = 8
VMEM_LIMIT = 56 * 1024 * 1024

NT_DIMS = (((1,), (1,)), ((), ()))
TN_DIMS = (((0,), (0,)), ((), ()))


def _params(*sem):
    return pltpu.CompilerParams(dimension_semantics=sem, vmem_limit_bytes=VMEM_LIMIT)


def _resident(shape, index_map):
    return pl.BlockSpec(shape, index_map, pipeline_mode=pl.Buffered(1))


def _rms(x, g):
    return x * lax.rsqrt(jnp.mean(x * x, axis=-1, keepdims=True) + EPS) * g


def _split3(x):
    hi = x.astype(BF16)
    r = x - hi.astype(F32)
    mid = r.astype(BF16)
    lo = (r - mid.astype(F32)).astype(BF16)
    return hi, mid, lo


def _dot01_left(m01, x):
    return sum(jnp.dot(m01, t, preferred_element_type=F32) for t in _split3(x))


def _dot01_right(x, m01):
    return sum(jnp.dot(t, m01, preferred_element_type=F32) for t in _split3(x))


def _ffn_kernel(x_ref, g_ref, wa_ref, wb_ref, wo_ref, *rest, final_norm):
    if final_norm:
        gf_ref, o_ref = rest
    else:
        (o_ref,) = rest
    x = x_ref[...]
    xn = _rms(x, g_ref[...]).astype(BF16)
    a = jnp.dot(xn, wa_ref[...], preferred_element_type=F32)
    b = jnp.dot(xn, wb_ref[...], preferred_element_type=F32)
    h = (a * jax.nn.sigmoid(a) * b).astype(BF16)
    y = x + 0.5 * jnp.dot(h, wo_ref[...], preferred_element_type=F32)
    if final_norm:
        y = _rms(y, gf_ref[...])
    o_ref[...] = y


def _ffn(x, g, w_in, w_out, layer, final_g=None, tm=512):
    n, d = x.shape
    ff = w_out.shape[1]
    tm = min(tm, n)
    in_specs = [
        pl.BlockSpec((tm, d), lambda i: (i, 0)),
        _resident((1, d), lambda i: (0, 0)),
        _resident((None, d, ff), lambda i: (layer, 0, 0)),
        _resident((None, d, ff), lambda i: (layer, 0, 1)),
        _resident((None, ff, d), lambda i: (layer, 0, 0)),
    ]
    args = [x, g.reshape(1, d), w_in, w_in, w_out]
    if final_g is not None:
        in_specs.append(_resident((1, d), lambda i: (0, 0)))
        args.append(final_g.reshape(1, d))
    return pl.pallas_call(
        functools.partial(_ffn_kernel, final_norm=final_g is not None),
        grid=(n // tm,),
        in_specs=in_specs,
        out_specs=pl.BlockSpec((tm, d), lambda i: (i, 0)),
        out_shape=jax.ShapeDtypeStruct((n, d), F32),
        compiler_params=_params("parallel"),
        name="ffn",
    )(*args)


def _inproj_kernel(x_ref, g_ref, w_ref, wf_ref, fb_ref,
                   hg_ref, q_ref, k_ref, v_ref, lf_ref, cum_ref, cumt_ref, *rest,
                   seq_len, d_hg, d_fox, feature_major):
    if feature_major:
        kt_ref, vt_ref, carry_ref = rest
    else:
        (carry_ref,) = rest
    tm = x_ref.shape[0]
    i = pl.program_id(0)
    xn = _rms(x_ref[...], g_ref[...]).astype(BF16)
    p = jnp.dot(xn, w_ref[...], preferred_element_type=F32)
    hg_ref[...] = p[:, :d_hg]
    q_ref[...] = p[:, d_hg:d_hg + d_fox]
    k = p[:, d_hg + d_fox:d_hg + 2 * d_fox]
    v = p[:, d_hg + 2 * d_fox:d_hg + 3 * d_fox]
    k_ref[...] = k
    v_ref[...] = v
    if feature_major:
        kt_ref[0] = k.T
        vt_ref[0] = v.T

    z = jnp.dot(xn, wf_ref[...], preferred_element_type=F32) + fb_ref[...]
    lf = jnp.minimum(z, 0.0) - jnp.log1p(jnp.exp(-jnp.abs(z)))

    row = lax.broadcasted_iota(jnp.int32, (tm, tm), 0)
    col = lax.broadcasted_iota(jnp.int32, (tm, tm), 1)
    same = col <= row
    if seq_len < tm:
        same = same & ((row // seq_len) == (col // seq_len))
    cum = _dot01_left(same.astype(BF16), lf)
    if seq_len > tm:
        tiles_per_seq = seq_len // tm

        @pl.when(i % tiles_per_seq == 0)
        def _():
            carry_ref[...] = jnp.zeros_like(carry_ref)

        cum = cum + carry_ref[...]
        carry_ref[...] = cum[tm - 1:tm, :]
    lf_ref[...] = lf[:, :H_FOX]
    cum_ref[...] = cum[:, :H_FOX]
    cumt_ref[0] = cum.T[:H_FOX, :]


def _inproj(x, g, w_all, w_f, f_bias, layer, seq_len, d_hg, d_fox, tm=512):
    n, d = x.shape
    tm = min(tm, n)
    assert seq_len % tm == 0 or tm % seq_len == 0
    d_main = d_hg + 3 * d_fox
    nt = n // tm
    tok = lambda w: pl.BlockSpec((tm, w), lambda i: (i, 0))
    out_specs = [tok(d_hg), tok(d_fox), tok(d_fox), tok(d_fox), tok(H_FOX), tok(H_FOX),
                 pl.BlockSpec((1, H_FOX, tm), lambda i: (i, 0, 0))]
    out_shape = [
        jax.ShapeDtypeStruct((n, d_hg), F32),
        jax.ShapeDtypeStruct((n, d_fox), F32),
        jax.ShapeDtypeStruct((n, d_fox), F32),
        jax.ShapeDtypeStruct((n, d_fox), F32),
        jax.ShapeDtypeStruct((n, H_FOX), F32),
        jax.ShapeDtypeStruct((n, H_FOX), F32),
        jax.ShapeDtypeStruct((nt, H_FOX, tm), F32),
    ]
    feature_major = seq_len % tm == 0
    if feature_major:
        tps = seq_len // tm
        out_specs += [pl.BlockSpec((1, d_fox, tm), lambda i: (i // tps, 0, i % tps))] * 2
        out_shape += [jax.ShapeDtypeStruct((n // seq_len, d_fox, seq_len), F32)] * 2
    return pl.pallas_call(
        functools.partial(_inproj_kernel, seq_len=seq_len, d_hg=d_hg, d_fox=d_fox,
                          feature_major=feature_major),
        grid=(nt,),
        in_specs=[
            tok(d),
            _resident((1, d), lambda i: (0, 0)),
            _resident((None, d, d_main), lambda i: (layer, 0, 0)),
            _resident((None, d, LANES), lambda i: (layer, 0, 0)),
            _resident((None, 1, LANES), lambda i: (layer, 0, 0)),
        ],
        out_specs=out_specs,
        out_shape=out_shape,
        scratch_shapes=[pltpu.VMEM((1, LANES), F32)],
        compiler_params=_params("arbitrary"),
        name="inproj",
    )(x, g.reshape(1, d), w_all, w_f, f_bias)


def _hgrn_kernel(lbp_ref, aq_ref, af_ref, ai_ref, ag_ref, gain_ref, *rest,
                 layer, chunk, sub, valid_len, zero_init):
    if zero_init:
        o_ref, sout_ref, st_ref = rest
    else:
        s0_ref, o_ref, sout_ref, st_ref = rest
    tb = aq_ref.shape[1]
    nh = aq_ref.shape[2] // LANES
    nchunks = tb // chunk
    t = pl.program_id(1)

    @pl.when(t == 0)
    def _():
        for h in range(nh):
            if zero_init:
                st_ref[h] = jnp.zeros((LANES, LANES), F32)
            else:
                st_ref[h] = s0_ref[0, 0, h].T

    lbp = lbp_ref[...]
    e = jnp.exp(lbp - jnp.max(lbp, axis=0, keepdims=True))
    prob = e / jnp.sum(e, axis=0, keepdims=True)
    lb = jnp.zeros((1, nh * LANES), F32)
    for j in range(1, layer + 1):
        lb = lb + prob[j:j + 1, :]

    row = lax.broadcasted_iota(jnp.int32, (chunk, 1), 0)
    rmod = row % sub
    tri = (lax.broadcasted_iota(jnp.int32, (chunk, chunk), 1)
           <= lax.broadcasted_iota(jnp.int32, (chunk, chunk), 0)).astype(BF16)
    gain_all = gain_ref[...]

    def chunk_body(c, carry):
        r0 = pl.multiple_of(c * chunk, chunk)
        rows = pl.ds(r0, chunk)
        f_all = lb + (1.0 - lb) * jax.nn.sigmoid(af_ref[0, rows, :])
        logf_all = jnp.log(jnp.maximum(f_all, TINY))
        k_all = 1.0 - f_all
        if valid_len is not None:
            live = (t * tb + r0 + row) < valid_len
            logf_all = jnp.where(live, logf_all, 0.0)
            k_all = jnp.where(live, k_all, 0.0)
        a_all = _dot01_left(tri, logf_all)

        for h in range(nh):
            hs = slice(h * LANES, (h + 1) * LANES)
            q = aq_ref[0, rows, hs]
            v = ai_ref[0, rows, hs]
            g = ag_ref[0, rows, hs]
            k, a = k_all[:, hs], a_all[:, hs]
            st = st_ref[h]
            v_bf = v.astype(BF16)

            o = lax.dot_general((q * jnp.exp(a)).astype(BF16), st.astype(BF16), NT_DIMS,
                                preferred_element_type=F32)

            for d in range(sub):
                if d == 0:
                    kd, ad, vd = k, a, v
                else:
                    kd = pltpu.roll(k, d, 0)
                    ad = pltpu.roll(a, d, 0)
                    vd = pltpu.roll(v, d, 0)
                ok = rmod >= d
                w = q * kd * jnp.exp(jnp.where(ok, a - ad, 0.0))
                sc = jnp.where(ok, jnp.sum(w, axis=-1, keepdims=True), 0.0)
                o = o + sc * vd

            pieces = [jnp.zeros((sub, LANES), F32)]
            for i in range(1, chunk // sub):
                lo, hi = i * sub, (i + 1) * sub
                r = a[lo - 1:lo, :]
                qt = (q[lo:hi] * jnp.exp(a[lo:hi] - r)).astype(BF16)
                kt = (k[:lo] * jnp.exp(r - a[:lo])).astype(BF16)
                sc = lax.dot_general(qt, kt, NT_DIMS, preferred_element_type=F32)
                pieces.append(jnp.dot(sc.astype(BF16), v_bf[:lo], preferred_element_type=F32))
            if len(pieces) > 1:
                o = o + jnp.concatenate(pieces, axis=0)

            a_last = a[chunk - 1:chunk, :]
            kt = (k * jnp.exp(a_last - a)).astype(BF16)
            st_ref[h] = st * jnp.exp(a_last) + lax.dot_general(
                v_bf, kt, TN_DIMS, preferred_element_type=F32)

            y = o * lax.rsqrt(jnp.mean(o * o, axis=-1, keepdims=True) + EPS) * gain_all[:, hs]
            y = y * (g * jax.nn.sigmoid(g))
            o_ref[0, rows, hs] = y.astype(o_ref.dtype)
        return carry

    lax.fori_loop(0, nchunks, chunk_body, 0)

    @pl.when(t == pl.num_programs(1) - 1)
    def _():
        for h in range(nh):
            sout_ref[0, h] = st_ref[h].T


def _hgrn(hg, lb_param, gain, state0, layer, *, tb, chunk, sub, valid_len, out_dtype):
    bn, t_len, d4 = hg.shape
    dh = d4 // 4
    nh = dh // LANES
    depth = lb_param.shape[0]
    blk = lambda c: pl.BlockSpec((1, tb, dh), lambda b, t: (b, t, c))
    in_specs = [pl.BlockSpec((depth, dh), lambda b, t: (0, 0)),
                blk(0), blk(1), blk(2), blk(3),
                pl.BlockSpec((1, dh), lambda b, t: (0, 0))]
    args = [lb_param, hg, hg, hg, hg, gain.reshape(1, dh)]
    if state0 is not None:
        in_specs.append(pl.BlockSpec((1, 1, nh, LANES, LANES), lambda b, t: (layer, b, 0, 0, 0)))
        args.append(state0)
    return pl.pallas_call(
        functools.partial(_hgrn_kernel, layer=layer, chunk=chunk, sub=sub,
                          valid_len=valid_len, zero_init=state0 is None),
        grid=(bn, t_len // tb),
        in_specs=in_specs,
        out_specs=[pl.BlockSpec((1, tb, dh), lambda b, t: (b, t, 0)),
                   pl.BlockSpec((1, nh, LANES, LANES), lambda b, t: (b, 0, 0, 0))],
        out_shape=[jax.ShapeDtypeStruct((bn, t_len, dh), out_dtype),
                   jax.ShapeDtypeStruct((bn, nh, LANES, LANES), F32)],
        scratch_shapes=[pltpu.VMEM((nh, LANES, LANES), F32)],
        compiler_params=_params("parallel", "arbitrary"),
        name="hgrn",
    )(*args)


def _fox_prompt_kernel(q_ref, k_ref, v_ref, cum_ref, cumt_ref, gain_ref, o_ref,
                       s_ref, p_ref, fill_ref, m_ref, l_ref, acc_ref, oh_ref, *, scale):
    tq = q_ref.shape[1]
    rc = FOX_ROW_CHUNK
    dh = LANES // 2
    hp = pl.program_id(1)
    qi = pl.program_id(2)
    lane = lax.broadcasted_iota(jnp.int32, (1, LANES), 1)
    q = q_ref[0] * (scale * LOG2E)
    cum = cum_ref[0]
    hl = lax.broadcasted_iota(jnp.int32, (1, H_FOX), 1)

    for hh in range(2):
        h = 2 * hp + hh
        qm = jnp.where((lane // dh) == hh, q, 0.0).astype(BF16)
        cq = jnp.sum(jnp.where(hl == h, cum, 0.0), axis=-1, keepdims=True)
        fill_ref[...] = jnp.broadcast_to((NEG - cq) * LOG2E, (tq, LANES))
        m_ref[...] = jnp.full((tq, LANES), -jnp.inf, F32)
        l_ref[...] = jnp.zeros((tq, LANES), F32)
        acc_ref[...] = jnp.zeros((tq, LANES), F32)

        def key_bias(kb):
            k0 = pl.multiple_of(kb * tq, tq)
            return cumt_ref[0, pl.ds(h, 1), pl.ds(k0, tq)] * LOG2E

        def scores(kb, masked):
            k0 = pl.multiple_of(kb * tq, tq)
            k = k_ref[0, pl.ds(k0, tq), :].astype(BF16)
            s_ref[kb] = lax.dot_general(qm, k, NT_DIMS, preferred_element_type=F32)
            ck = key_bias(kb)
            for r in range(tq // rc):
                rs = slice(r * rc, (r + 1) * rc)
                m = m_ref[rs, :]
                for j in range(tq // LANES):
                    cs = slice(j * LANES, (j + 1) * LANES)
                    if masked and j * LANES > (r + 1) * rc - 1:
                        continue
                    s = s_ref[kb, rs, cs] - ck[:, cs]
                    if masked and (j + 1) * LANES - 1 > r * rc:
                        rid = r * rc + lax.broadcasted_iota(jnp.int32, (rc, LANES), 0)
                        cid = j * LANES + lax.broadcasted_iota(jnp.int32, (rc, LANES), 1)
                        s = jnp.where(rid >= cid, s, fill_ref[rs, :])
                    if masked:
                        s_ref[kb, rs, cs] = s
                    m = jnp.maximum(m, s)
                m_ref[rs, :] = m

        def weights(kb, masked):
            k0 = pl.multiple_of(kb * tq, tq)
            ck = key_bias(kb)
            for r in range(tq // rc):
                rs = slice(r * rc, (r + 1) * rc)
                m = m_ref[rs, :]
                lsum = l_ref[rs, :]
                for j in range(tq // LANES):
                    cs = slice(j * LANES, (j + 1) * LANES)
                    if masked and j * LANES > (r + 1) * rc - 1:
                        p_ref[rs, cs] = jnp.zeros((rc, LANES), BF16)
                        continue
                    s = s_ref[kb, rs, cs]
                    if not masked:
                        s = s - ck[:, cs]
                    p = jnp.exp2(s - m)
                    lsum = lsum + p
                    p_ref[rs, cs] = p.astype(BF16)
                l_ref[rs, :] = lsum
            v = v_ref[0, pl.ds(k0, tq), :].astype(BF16)
            acc_ref[...] += jnp.dot(p_ref[...], v, preferred_element_type=F32)

        def loop(fn):
            def body(kb, carry):
                fn(kb, masked=False)
                return carry
            lax.fori_loop(0, qi, body, 0)
            fn(qi, masked=True)

        loop(scores)
        m_ref[...] = jnp.broadcast_to(jnp.max(m_ref[...], axis=-1, keepdims=True), (tq, LANES))
        loop(weights)
        oh_ref[hh] = acc_ref[...] / jnp.sum(l_ref[...], axis=-1, keepdims=True)

    low = (lane // dh) == 0
    o = jnp.where(low, oh_ref[0], oh_ref[1])
    sq = o * o
    ss0 = jnp.sum(jnp.where(low, sq, 0.0), axis=-1, keepdims=True)
    ss1 = jnp.sum(jnp.where(low, 0.0, sq), axis=-1, keepdims=True)
    inv = jnp.where(low, lax.rsqrt(ss0 / dh + EPS), lax.rsqrt(ss1 / dh + EPS))
    o_ref[0] = (o * inv * gain_ref[...]).astype(o_ref.dtype)


def _fox_prompt(q, k, v, cum, cumt, gain, tq=512):
    bn, t_len, d = q.shape
    tq = min(tq, t_len)
    nhp = d // LANES
    scale = float((LANES // 2) ** -0.5)
    return pl.pallas_call(
        functools.partial(_fox_prompt_kernel, scale=scale),
        grid=(bn, nhp, t_len // tq),
        in_specs=[
            pl.BlockSpec((1, tq, LANES), lambda b, h, i: (b, i, h)),
            pl.BlockSpec((1, t_len, LANES), lambda b, h, i: (b, 0, h)),
            pl.BlockSpec((1, t_len, LANES), lambda b, h, i: (b, 0, h)),
            pl.BlockSpec((1, tq, H_FOX), lambda b, h, i: (b, i, 0)),
            pl.BlockSpec((1, H_FOX, t_len), lambda b, h, i: (b, 0, 0)),
            pl.BlockSpec((1, LANES), lambda b, h, i: (0, h)),
        ],
        out_specs=pl.BlockSpec((1, tq, LANES), lambda b, h, i: (b, i, h)),
        out_shape=jax.ShapeDtypeStruct((bn, t_len, d), BF16),
        scratch_shapes=[pltpu.VMEM((t_len // tq, tq, tq), F32),
                        pltpu.VMEM((tq, tq), BF16),
                        pltpu.VMEM((tq, LANES), F32),
                        pltpu.VMEM((tq, LANES), F32),
                        pltpu.VMEM((tq, LANES), F32),
                        pltpu.VMEM((tq, LANES), F32),
                        pltpu.VMEM((2, tq, LANES), F32)],
        compiler_params=_params("parallel", "parallel", "arbitrary"),
        name="fox_prompt",
    )(q, k, v, cum, cumt, gain.reshape(1, d))


def _suffix_kernel(lf_ref, suf_ref, tot_ref):
    n = lf_ref.shape[1]
    i = lax.broadcasted_iota(jnp.int32, (n, n), 0)
    j = lax.broadcasted_iota(jnp.int32, (n, n), 1)
    lf = lf_ref[...]
    suf_ref[...] = _dot01_right(lf, (i > j).astype(BF16))
    tot_ref[...] = _dot01_right(lf, jnp.ones((n, n), BF16))


def _suffix(lf_t, tr=1024):
    rows, n = lf_t.shape
    tr = min(tr, rows)
    assert rows % tr == 0
    spec = pl.BlockSpec((tr, n), lambda i: (i, 0))
    return pl.pallas_call(
        _suffix_kernel,
        grid=(rows // tr,),
        in_specs=[spec],
        out_specs=[spec, spec],
        out_shape=[jax.ShapeDtypeStruct((rows, n), F32)] * 2,
        compiler_params=_params("parallel"),
        name="suffix",
    )(lf_t)


def _fox_decode_kernel(pt_ref, q_ref, kn_ref, vn_ref, cq_ref, gain_ref, *rest, pps, scale):
    k_refs = rest[:pps]
    v_refs = rest[pps:2 * pps]
    suf_refs = rest[2 * pps:3 * pps]
    tot_refs = rest[3 * pps:4 * pps]
    o_ref, m_ref, l_ref, acc_ref, carry_ref, qbd_ref = rest[4 * pps:]
    t_new = q_ref.shape[1]
    d = q_ref.shape[2]
    dh = d // H_FOX
    rows = t_new * H_FOX
    page = k_refs[0].shape[2]
    j = pl.program_id(1)
    head_of_lane = lax.broadcasted_iota(jnp.int32, (H_FOX, d), 1) // dh
    head_of_row = lax.broadcasted_iota(jnp.int32, (H_FOX, d), 0)
    diag = head_of_lane == head_of_row
    cq = cq_ref[0]

    @pl.when(j == 0)
    def _():
        qbd = jnp.concatenate(
            [jnp.where(diag, q_ref[0, t:t + 1, :] * scale, 0.0) for t in range(t_new)], axis=0)
        qbd_ref[...] = qbd
        qrow = lax.broadcasted_iota(jnp.int32, (rows, 1), 0) // H_FOX
        s_new = []
        for s in range(t_new):
            dot_s = jnp.sum(qbd * kn_ref[0, s:s + 1, :], axis=-1, keepdims=True)
            cs = jnp.concatenate([cq[s * H_FOX:(s + 1) * H_FOX]] * t_new, axis=0)
            s_new.append(jnp.where(qrow >= s, dot_s + cq - cs, NEG))
        m0 = s_new[0]
        for s in range(1, t_new):
            m0 = jnp.maximum(m0, s_new[s])
        l0 = jnp.zeros((rows, 1), F32)
        acc0 = jnp.zeros((rows, d), F32)
        for s in range(t_new):
            p_s = jnp.exp(s_new[s] - m0)
            l0 = l0 + p_s
            acc0 = acc0 + p_s * vn_ref[0, s:s + 1, :]
        m_ref[...] = jnp.broadcast_to(m0, (rows, LANES))
        l_ref[...] = jnp.broadcast_to(l0, (rows, LANES))
        acc_ref[...] = acc0
        carry_ref[...] = jnp.zeros_like(carry_ref)

    qbd = qbd_ref[...].astype(BF16)
    carry = carry_ref[...]
    s_parts = []
    for i in range(pps):
        kpg = k_refs[i][0].astype(BF16)
        s = jnp.dot(qbd, kpg, preferred_element_type=F32)
        suf = suf_refs[i][0] + carry
        carry = carry + tot_refs[i][0]
        s_parts.append(s + jnp.concatenate([suf] * t_new, axis=0) + cq)
    carry_ref[...] = carry
    s_all = jnp.concatenate(s_parts, axis=1)
    m_prev = m_ref[...]
    m_new = jnp.maximum(m_prev, jnp.max(s_all, axis=-1, keepdims=True))
    alpha = jnp.exp(m_prev - m_new)
    p = jnp.exp(s_all - m_new[:, :1])
    l_ref[...] = alpha * l_ref[...] + jnp.sum(p, axis=-1, keepdims=True)
    m_ref[...] = m_new
    vt = jnp.concatenate([v_refs[i][0] for i in range(pps)], axis=1).astype(BF16)
    pv = lax.dot_general(p.astype(BF16), vt, NT_DIMS, preferred_element_type=F32)
    acc_ref[...] = alpha[:, :1] * acc_ref[...] + pv

    @pl.when(j == pl.num_programs(1) - 1)
    def _():
        o = acc_ref[...] / l_ref[...][:, :1]
        o = jnp.where(jnp.concatenate([diag] * t_new, axis=0), o, 0.0)
        y = o * lax.rsqrt(jnp.sum(o * o, axis=-1, keepdims=True) / dh + EPS)
        out = jnp.concatenate(
            [jnp.sum(y[t * H_FOX:(t + 1) * H_FOX], axis=0, keepdims=True) for t in range(t_new)],
            axis=0)
        o_ref[0] = out * gain_ref[...]


def _fox_decode(q, k_new, v_new, cq_col, gain, cache_k, cache_v, suf, tot, page_table,
                layer, n_pool, pps=8):
    bn, t_new, d = q.shape
    page = cache_k.shape[2]
    n_pages = page_table.shape[1]
    pps = min(pps, n_pages)
    assert n_pages % pps == 0
    rows = t_new * H_FOX
    scale = float((d // H_FOX) ** -0.5)
    base = layer * n_pool

    def page_idx(i):
        return lambda b, j, pt: (base + pt[b, n_pages - 1 - (j * pps + i)], 0, 0)

    tok = pl.BlockSpec((1, t_new, d), lambda b, j, pt: (b, 0, 0))
    in_specs = [tok, tok, tok,
                pl.BlockSpec((1, rows, 1), lambda b, j, pt: (b, 0, 0)),
                pl.BlockSpec((1, d), lambda b, j, pt: (0, 0))]
    in_specs += [pl.BlockSpec((1, d, page), page_idx(i)) for i in range(pps)]
    in_specs += [pl.BlockSpec((1, d, page), page_idx(i)) for i in range(pps)]
    in_specs += [pl.BlockSpec((1, H_FOX, page), page_idx(i)) for i in range(pps)]
    in_specs += [pl.BlockSpec((1, H_FOX, page), page_idx(i)) for i in range(pps)]
    grid_spec = pltpu.PrefetchScalarGridSpec(
        num_scalar_prefetch=1,
        grid=(bn, n_pages // pps),
        in_specs=in_specs,
        out_specs=pl.BlockSpec((1, t_new, d), lambda b, j, pt: (b, 0, 0)),
        scratch_shapes=[pltpu.VMEM((rows, LANES), F32),
                        pltpu.VMEM((rows, LANES), F32),
                        pltpu.VMEM((rows, d), F32),
                        pltpu.VMEM((H_FOX, page), F32),
                        pltpu.VMEM((rows, d), F32)],
    )
    return pl.pallas_call(
        functools.partial(_fox_decode_kernel, pps=pps, scale=scale),
        grid_spec=grid_spec,
        out_shape=jax.ShapeDtypeStruct((bn, t_new, d), F32),
        compiler_params=_params("parallel", "arbitrary"),
        name="fox_decode",
    )(page_table, q, k_new, v_new, cq_col, gain.reshape(1, d),
      *([cache_k] * pps), *([cache_v] * pps), *([suf] * pps), *([tot] * pps))


def _outproj_kernel(x_ref, oa_ref, of_ref, wa_ref, wf_ref, o_ref):
    o_ref[...] = (x_ref[...]
                  + jnp.dot(oa_ref[...], wa_ref[...], preferred_element_type=F32)
                  + jnp.dot(of_ref[...], wf_ref[...], preferred_element_type=F32))


def _outproj(x, o_a, o_f, w_out, layer, tm=512):
    n, d = x.shape
    da, df = o_a.shape[1], o_f.shape[1]
    assert da == df
    tm = min(tm, n)
    return pl.pallas_call(
        _outproj_kernel,
        grid=(n // tm,),
        in_specs=[
            pl.BlockSpec((tm, d), lambda i: (i, 0)),
            pl.BlockSpec((tm, da), lambda i: (i, 0)),
            pl.BlockSpec((tm, df), lambda i: (i, 0)),
            _resident((None, da, d), lambda i: (layer, 0, 0)),
            _resident((None, df, d), lambda i: (layer, 1, 0)),
        ],
        out_specs=pl.BlockSpec((tm, d), lambda i: (i, 0)),
        out_shape=jax.ShapeDtypeStruct((n, d), F32),
        compiler_params=_params("parallel"),
        name="outproj",
    )(x, o_a, o_f, w_out, w_out)


def kernel(x_prompt, x_sample, cache_k, cache_v, cache_logf, state_hgrn, page_table, norm_ffn1, ffn1_w_in, ffn1_w_out, norm_mix, w_in_mix, hgrn_lb, fox_f_bias, hgrn_out_gain, fox_out_gain, w_out_mix, norm_ffn2, ffn2_w_in, ffn2_w_out, norm_final):
    depth = norm_ffn1.shape[0]
    bn, t_len, d = x_prompt.shape
    db, t_new, _ = x_sample.shape
    d_fox = fox_out_gain.shape[1]
    d_h = hgrn_out_gain.shape[1]
    dh_fox = d_fox // H_FOX
    n_pool, page = cache_k.shape[1], cache_k.shape[2]
    d_main = 4 * d_h + 3 * d_fox

    w1_in, w1_out = ffn1_w_in.astype(BF16), ffn1_w_out.astype(BF16)
    w2_in, w2_out = ffn2_w_in.astype(BF16), ffn2_w_out.astype(BF16)
    w_mix = w_in_mix.astype(BF16)
    w_f = jnp.pad(w_in_mix[:, :, d_main:], ((0, 0), (0, 0), (0, LANES - H_FOX))).astype(BF16)
    f_bias = jnp.pad(fox_f_bias, ((0, 0), (0, LANES - H_FOX))).reshape(depth, 1, LANES)
    w_out = w_out_mix.astype(BF16)

    ck = jnp.transpose(cache_k, (0, 1, 3, 4, 2)).reshape(depth * n_pool, d_fox, page)
    cv = jnp.transpose(cache_v, (0, 1, 3, 4, 2)).reshape(depth * n_pool, d_fox, page)
    lf_t = jnp.swapaxes(cache_logf, 2, 3).reshape(depth * n_pool * H_FOX, page)
    suf, tot = _suffix(lf_t)
    suf = suf.reshape(depth * n_pool, H_FOX, page)
    tot = tot.reshape(depth * n_pool, H_FOX, page)

    t_pad = -(-t_new // SUBLANES) * SUBLANES

    def run(x, seq_len, nseq, prompt):
        ks, vs, lfs, ss = [], [], [], []
        for l in range(depth):
            x = _ffn(x, norm_ffn1[l], w1_in, w1_out, l)
            hg, q, k, v, lf, cum, cumt, *kv_t = _inproj(x, norm_mix[l], w_mix, w_f, f_bias, l, seq_len,
                                                        4 * d_h, d_fox)
            if kv_t:
                k_out, v_out = (a.reshape(nseq, H_FOX, dh_fox, seq_len).transpose(0, 3, 1, 2) for a in kv_t)
            else:
                k_out, v_out = (a.reshape(nseq, seq_len, H_FOX, dh_fox) for a in (k, v))
            if prompt:
                o_a, s_fin = _hgrn(hg.reshape(nseq, seq_len, 4 * d_h), hgrn_lb, hgrn_out_gain[l], None, l,
                                   tb=min(512, seq_len), chunk=64, sub=16, valid_len=None, out_dtype=BF16)
                o_a = o_a.reshape(nseq * seq_len, d_h)
                tiles = cumt.shape[0] // nseq
                cumt_b = cumt.reshape(nseq, tiles, H_FOX, -1).transpose(0, 2, 1, 3).reshape(nseq, H_FOX, seq_len)
                o_f = _fox_prompt(q.reshape(nseq, seq_len, d_fox), k.reshape(nseq, seq_len, d_fox),
                                  v.reshape(nseq, seq_len, d_fox), cum.reshape(nseq, seq_len, H_FOX),
                                  cumt_b, fox_out_gain[l]).reshape(nseq * seq_len, d_fox)
            else:
                hg_p = jnp.pad(hg.reshape(nseq, seq_len, 4 * d_h), ((0, 0), (0, t_pad - seq_len), (0, 0)))
                o_a, s_fin = _hgrn(hg_p, hgrn_lb, hgrn_out_gain[l], state_hgrn, l,
                                   tb=t_pad, chunk=t_pad, sub=t_pad, valid_len=seq_len, out_dtype=F32)
                o_a = o_a[:, :seq_len].reshape(nseq * seq_len, d_h).astype(BF16)
                o_f = _fox_decode(q.reshape(nseq, seq_len, d_fox), k.reshape(nseq, seq_len, d_fox),
                                  v.reshape(nseq, seq_len, d_fox), cum.reshape(nseq, seq_len * H_FOX, 1),
                                  fox_out_gain[l], ck, cv, suf, tot, page_table, l, n_pool)
                o_f = o_f.reshape(nseq * seq_len, d_fox).astype(BF16)
            x = _outproj(x, o_a, o_f, w_out, l)
            x = _ffn(x, norm_ffn2[l], w2_in, w2_out, l,
                     final_g=norm_final if l == depth - 1 else None)
            ks.append(k_out)
            vs.append(v_out)
            lfs.append(lf.reshape(nseq, seq_len, H_FOX))
            ss.append(s_fin)
        return (x.reshape(nseq, seq_len, d), jnp.stack(ks), jnp.stack(vs), jnp.stack(lfs), jnp.stack(ss))

    y_p, k_p, v_p, lf_p, s_p = run(x_prompt.reshape(bn * t_len, d), t_len, bn, True)
    y_s, k_s, v_s, lf_s, s_s = run(x_sample.reshape(db * t_new, d), t_new, db, False)
    return (y_p, y_s, k_p, v_p, lf_p, s_p, k_s, v_s, lf_s, s_s)
```

```python
import functools

import jax
import jax.numpy as jnp
from jax import lax
from jax.experimental import pallas as pl
from jax.experimental.pallas import tpu as pltpu

F32 = jnp.float32
BF16 = jnp.bfloat16

EPS = 1e-6
TINY = 1e-30
NEG = -1e30

LOG2E = 1.4426950408889634
H_HGRN = 4
H_FOX = 8
FOX_ROW_CHUNK = 64
DECODE_GROUP = 2
LANES = 128
SUBLANES = 8
VMEM_LIMIT = 56 * 1024 * 1024

NT_DIMS = (((1,), (1,)), ((), ()))
TN_DIMS = (((0,), (0,)), ((), ()))


def _params(*sem):
    return pltpu.CompilerParams(dimension_semantics=sem, vmem_limit_bytes=VMEM_LIMIT)


def _resident(shape, index_map):
    return pl.BlockSpec(shape, index_map, pipeline_mode=pl.Buffered(1))


def _rms(x, g):
    return x * lax.rsqrt(jnp.mean(x * x, axis=-1, keepdims=True) + EPS) * g


def _split3(x):
    hi = x.astype(BF16)
    r = x - hi.astype(F32)
    mid = r.astype(BF16)
    lo = (r - mid.astype(F32)).astype(BF16)
    return hi, mid, lo


def _dot01_left(m01, x):
    return sum(jnp.dot(m01, t, preferred_element_type=F32) for t in _split3(x))


def _dot01_right(x, m01):
    return sum(jnp.dot(t, m01, preferred_element_type=F32) for t in _split3(x))


def _ffn_kernel(x_ref, g_ref, wa_ref, wb_ref, wo_ref, *rest, final_norm, mixer_out):
    rest = list(rest)
    o_ref = rest.pop()
    x = x_ref[...]
    if mixer_out:
        oa_ref, of_ref, wma_ref, wmf_ref = rest[:4]
        del rest[:4]
        x = (x + jnp.dot(oa_ref[...], wma_ref[...], preferred_element_type=F32)
             + jnp.dot(of_ref[...], wmf_ref[...], preferred_element_type=F32))
    if final_norm:
        (gf_ref,) = rest
    xn = _rms(x, g_ref[...]).astype(BF16)
    a = jnp.dot(xn, wa_ref[...], preferred_element_type=F32)
    b = jnp.dot(xn, wb_ref[...], preferred_element_type=F32)
    h = (a * jax.nn.sigmoid(a) * b).astype(BF16)
    y = x + 0.5 * jnp.dot(h, wo_ref[...], preferred_element_type=F32)
    if final_norm:
        y = _rms(y, gf_ref[...])
    o_ref[...] = y


def _ffn(x, g, w_in, w_out, layer, mixer=None, final_g=None, tm=512):
    n, d = x.shape
    ff = w_out.shape[1]
    tm = min(tm, n)
    in_specs = [
        pl.BlockSpec((tm, d), lambda i: (i, 0)),
        _resident((1, d), lambda i: (0, 0)),
        _resident((None, d, ff), lambda i: (layer, 0, 0)),
        _resident((None, d, ff), lambda i: (layer, 0, 1)),
        _resident((None, ff, d), lambda i: (layer, 0, 0)),
    ]
    args = [x, g.reshape(1, d), w_in, w_in, w_out]
    if mixer is not None:
        o_a, o_f, w_mix_out = mixer
        dm = o_a.shape[1]
        assert o_f.shape[1] == dm and w_mix_out.shape[1] == 2 * dm
        in_specs += [pl.BlockSpec((tm, dm), lambda i: (i, 0)),
                     pl.BlockSpec((tm, dm), lambda i: (i, 0)),
                     _resident((None, dm, d), lambda i: (layer, 0, 0)),
                     _resident((None, dm, d), lambda i: (layer, 1, 0))]
        args += [o_a, o_f, w_mix_out, w_mix_out]
    if final_g is not None:
        in_specs.append(_resident((1, d), lambda i: (0, 0)))
        args.append(final_g.reshape(1, d))
    return pl.pallas_call(
        functools.partial(_ffn_kernel, final_norm=final_g is not None, mixer_out=mixer is not None),
        grid=(n // tm,),
        in_specs=in_specs,
        out_specs=pl.BlockSpec((tm, d), lambda i: (i, 0)),
        out_shape=jax.ShapeDtypeStruct((n, d), F32),
        compiler_params=_params("parallel"),
        name="ffn",
    )(*args)


def _inproj_kernel(x_ref, g_ref, w_ref, wf_ref, fb_ref,
                   hg_ref, q_ref, k_ref, v_ref, lf_ref, cum_ref, cumt_ref, *rest,
                   seq_len, d_hg, d_fox, feature_major):
    if feature_major:
        kt_ref, vt_ref, carry_ref = rest
    else:
        (carry_ref,) = rest
    tm = x_ref.shape[0]
    i = pl.program_id(0)
    xn = _rms(x_ref[...], g_ref[...]).astype(BF16)
    p = jnp.dot(xn, w_ref[...], preferred_element_type=F32)
    hg_ref[...] = p[:, :d_hg]
    q_ref[...] = p[:, d_hg:d_hg + d_fox]
    k = p[:, d_hg + d_fox:d_hg + 2 * d_fox]
    v = p[:, d_hg + 2 * d_fox:d_hg + 3 * d_fox]
    k_ref[...] = k
    v_ref[...] = v
    if feature_major:
        kt_ref[0] = k.T
        vt_ref[0] = v.T

    z = jnp.dot(xn, wf_ref[...], preferred_element_type=F32) + fb_ref[...]
    lf = jnp.minimum(z, 0.0) - jnp.log1p(jnp.exp(-jnp.abs(z)))

    row = lax.broadcasted_iota(jnp.int32, (tm, tm), 0)
    col = lax.broadcasted_iota(jnp.int32, (tm, tm), 1)
    same = col <= row
    if seq_len < tm:
        same = same & ((row // seq_len) == (col // seq_len))
    cum = _dot01_left(same.astype(BF16), lf)
    if seq_len > tm:
        tiles_per_seq = seq_len // tm

        @pl.when(i % tiles_per_seq == 0)
        def _():
            carry_ref[...] = jnp.zeros_like(carry_ref)

        cum = cum + carry_ref[...]
        carry_ref[...] = cum[tm - 1:tm, :]
    lf_ref[...] = lf[:, :H_FOX]
    cum_ref[...] = cum[:, :H_FOX]
    cumt_ref[0] = cum.T[:H_FOX, :]


def _inproj(x, g, w_all, w_f, f_bias, layer, seq_len, d_hg, d_fox, tm=512):
    n, d = x.shape
    tm = min(tm, n)
    assert seq_len % tm == 0 or tm % seq_len == 0
    d_main = d_hg + 3 * d_fox
    nt = n // tm
    tok = lambda w: pl.BlockSpec((tm, w), lambda i: (i, 0))
    out_specs = [tok(d_hg), tok(d_fox), tok(d_fox), tok(d_fox), tok(H_FOX), tok(H_FOX),
                 pl.BlockSpec((1, H_FOX, tm), lambda i: (i, 0, 0))]
    out_shape = [
        jax.ShapeDtypeStruct((n, d_hg), F32),
        jax.ShapeDtypeStruct((n, d_fox), F32),
        jax.ShapeDtypeStruct((n, d_fox), F32),
        jax.ShapeDtypeStruct((n, d_fox), F32),
        jax.ShapeDtypeStruct((n, H_FOX), F32),
        jax.ShapeDtypeStruct((n, H_FOX), F32),
        jax.ShapeDtypeStruct((nt, H_FOX, tm), F32),
    ]
    feature_major = seq_len % tm == 0
    if feature_major:
        tps = seq_len // tm
        out_specs += [pl.BlockSpec((1, d_fox, tm), lambda i: (i // tps, 0, i % tps))] * 2
        out_shape += [jax.ShapeDtypeStruct((n // seq_len, d_fox, seq_len), F32)] * 2
    return pl.pallas_call(
        functools.partial(_inproj_kernel, seq_len=seq_len, d_hg=d_hg, d_fox=d_fox,
                          feature_major=feature_major),
        grid=(nt,),
        in_specs=[
            tok(d),
            _resident((1, d), lambda i: (0, 0)),
            _resident((None, d, d_main), lambda i: (layer, 0, 0)),
            _resident((None, d, LANES), lambda i: (layer, 0, 0)),
            _resident((None, 1, LANES), lambda i: (layer, 0, 0)),
        ],
        out_specs=out_specs,
        out_shape=out_shape,
        scratch_shapes=[pltpu.VMEM((1, LANES), F32)],
        compiler_params=_params("arbitrary"),
        name="inproj",
    )(x, g.reshape(1, d), w_all, w_f, f_bias)


def _hgrn_kernel(lbp_ref, aq_ref, af_ref, ai_ref, ag_ref, gain_ref, *rest,
                 layer, chunk, sub, valid_len, zero_init):
    if zero_init:
        o_ref, sout_ref, st_ref, ks_ref, as_ref = rest
    else:
        s0_ref, o_ref, sout_ref, st_ref, ks_ref, as_ref = rest
    tb = aq_ref.shape[1]
    nh = aq_ref.shape[2] // LANES
    nchunks = tb // chunk
    t = pl.program_id(1)

    @pl.when(t == 0)
    def _():
        for h in range(nh):
            if zero_init:
                st_ref[h] = jnp.zeros((LANES, LANES), F32)
            else:
                st_ref[h] = s0_ref[0, 0, h].T

    lbp = lbp_ref[...]
    e = jnp.exp(lbp - jnp.max(lbp, axis=0, keepdims=True))
    prob = e / jnp.sum(e, axis=0, keepdims=True)
    lb = jnp.zeros((1, nh * LANES), F32)
    for j in range(1, layer + 1):
        lb = lb + prob[j:j + 1, :]

    nsub = chunk // sub
    row = lax.broadcasted_iota(jnp.int32, (chunk, 1), 0)
    rmod = row % sub
    lane_off = lax.broadcasted_iota(jnp.int32, (chunk, LANES), 1) - (row - rmod)
    tri = (lax.broadcasted_iota(jnp.int32, (chunk, chunk), 1)
           <= lax.broadcasted_iota(jnp.int32, (chunk, chunk), 0)).astype(BF16)
    zpad_bf = jnp.zeros((LANES - chunk, LANES), BF16)
    gain_all = gain_ref[...]

    def chunk_body(c, carry):
        r0 = pl.multiple_of(c * chunk, chunk)
        rows = pl.ds(r0, chunk)
        f_all = lb + (1.0 - lb) * jax.nn.sigmoid(af_ref[0, rows, :])
        logf_all = jnp.log(jnp.maximum(f_all, TINY))
        k_all = 1.0 - f_all
        if valid_len is not None:
            live = (t * tb + r0 + row) < valid_len
            logf_all = jnp.where(live, logf_all, 0.0)
            k_all = jnp.where(live, k_all, 0.0)
        a_all = _dot01_left(tri, logf_all) * LOG2E
        ks_ref[...] = k_all
        as_ref[...] = a_all

        for h in range(nh):
            hs = slice(h * LANES, (h + 1) * LANES)
            q = aq_ref[0, rows, hs]
            v = ai_ref[0, rows, hs]
            g = ag_ref[0, rows, hs]
            k, a = k_all[:, hs], a_all[:, hs]
            st = st_ref[h]
            v_bf = v.astype(BF16)

            o = lax.dot_general((q * jnp.exp2(a)).astype(BF16), st.astype(BF16), NT_DIMS,
                                preferred_element_type=F32)

            blocks = [jnp.zeros((sub, LANES), F32)]
            for i in range(1, nsub):
                lo, hi = i * sub, (i + 1) * sub
                r = a[lo - 1:lo, :]
                qt = (q[lo:hi] * jnp.exp2(a[lo:hi] - r)).astype(BF16)
                kt = jnp.where(row < lo, k * jnp.exp2(r - a), 0.0).astype(BF16)
                blocks.append(lax.dot_general(qt, jnp.concatenate([kt, zpad_bf], axis=0), NT_DIMS,
                                              preferred_element_type=F32))
            sc = jnp.concatenate(blocks, axis=0) if nsub > 1 else blocks[0]
            for j in range(sub):
                kj = jnp.concatenate([jnp.broadcast_to(ks_ref[i * sub + j:i * sub + j + 1, hs], (sub, LANES))
                                      for i in range(nsub)], axis=0)
                aj = jnp.concatenate([jnp.broadcast_to(as_ref[i * sub + j:i * sub + j + 1, hs], (sub, LANES))
                                      for i in range(nsub)], axis=0)
                w = q * kj * jnp.exp2(a - aj)
                col = jnp.where(rmod >= j, jnp.sum(w, axis=-1, keepdims=True), 0.0)
                sc = jnp.where(lane_off == j, col, sc)
            o = o + jnp.dot(sc.astype(BF16), jnp.concatenate([v_bf, zpad_bf], axis=0),
                            preferred_element_type=F32)

            a_last = a[chunk - 1:chunk, :]
            kt = (k * jnp.exp2(a_last - a)).astype(BF16)
            st_ref[h] = st * jnp.exp2(a_last) + lax.dot_general(
                v_bf, kt, TN_DIMS, preferred_element_type=F32)

            y = o * lax.rsqrt(jnp.mean(o * o, axis=-1, keepdims=True) + EPS) * gain_all[:, hs]
            y = y * (g * jax.nn.sigmoid(g))
            o_ref[0, rows, hs] = y.astype(o_ref.dtype)
        return carry

    lax.fori_loop(0, nchunks, chunk_body, 0)

    @pl.when(t == pl.num_programs(1) - 1)
    def _():
        for h in range(nh):
            sout_ref[0, h] = st_ref[h].T


def _hgrn(hg, lb_param, gain, state0, layer, *, tb, chunk, sub, valid_len, out_dtype):
    bn, t_len, d4 = hg.shape
    dh = d4 // 4
    nh = dh // LANES
    depth = lb_param.shape[0]
    blk = lambda c: pl.BlockSpec((1, tb, dh), lambda b, t: (b, t, c))
    in_specs = [pl.BlockSpec((depth, dh), lambda b, t: (0, 0)),
                blk(0), blk(1), blk(2), blk(3),
                pl.BlockSpec((1, dh), lambda b, t: (0, 0))]
    args = [lb_param, hg, hg, hg, hg, gain.reshape(1, dh)]
    if state0 is not None:
        in_specs.append(pl.BlockSpec((1, 1, nh, LANES, LANES), lambda b, t: (layer, b, 0, 0, 0)))
        args.append(state0)
    return pl.pallas_call(
        functools.partial(_hgrn_kernel, layer=layer, chunk=chunk, sub=sub,
                          valid_len=valid_len, zero_init=state0 is None),
        grid=(bn, t_len // tb),
        in_specs=in_specs,
        out_specs=[pl.BlockSpec((1, tb, dh), lambda b, t: (b, t, 0)),
                   pl.BlockSpec((1, nh, LANES, LANES), lambda b, t: (b, 0, 0, 0))],
        out_shape=[jax.ShapeDtypeStruct((bn, t_len, dh), out_dtype),
                   jax.ShapeDtypeStruct((bn, nh, LANES, LANES), F32)],
        scratch_shapes=[pltpu.VMEM((nh, LANES, LANES), F32),
                        pltpu.VMEM((chunk, dh), F32),
                        pltpu.VMEM((chunk, dh), F32)],
        compiler_params=_params("parallel", "arbitrary"),
        name="hgrn",
    )(*args)


def _fox_prompt_kernel(q_ref, k_ref, v_ref, cum_ref, cumt_ref, gain_ref, o_ref,
                       s_ref, p_ref, fill_ref, m_ref, l_ref, acc_ref, *, scale):
    tq = q_ref.shape[1]
    rc = FOX_ROW_CHUNK
    dh = LANES // 2
    hp = pl.program_id(1)
    qi = pl.program_id(2)
    lane = lax.broadcasted_iota(jnp.int32, (1, LANES), 1)
    q = q_ref[0] * (scale * LOG2E)
    cum = cum_ref[0]
    hl = lax.broadcasted_iota(jnp.int32, (1, H_FOX), 1)

    heads = range(2)
    qms = []
    for hh in heads:
        qms.append(jnp.where((lane // dh) == hh, q, 0.0).astype(BF16))
        cq = jnp.sum(jnp.where(hl == 2 * hp + hh, cum, 0.0), axis=-1, keepdims=True)
        fill_ref[hh] = jnp.broadcast_to((NEG - cq) * LOG2E, (tq, LANES))
        m_ref[hh] = jnp.full((tq, LANES), -jnp.inf, F32)
        l_ref[hh] = jnp.zeros((tq, LANES), F32)
        acc_ref[hh] = jnp.zeros((tq, LANES), F32)

    def key_bias(hh, kb):
        k0 = pl.multiple_of(kb * tq, tq)
        return cumt_ref[0, pl.ds(2 * hp + hh, 1), pl.ds(k0, tq)] * LOG2E

    def scores(kb, masked):
        k0 = pl.multiple_of(kb * tq, tq)
        k = k_ref[0, pl.ds(k0, tq), :].astype(BF16)
        for hh in heads:
            s_ref[hh, kb] = lax.dot_general(qms[hh], k, NT_DIMS, preferred_element_type=F32)
        for hh in heads:
            ck = key_bias(hh, kb)
            for r in range(tq // rc):
                rs = slice(r * rc, (r + 1) * rc)
                m = m_ref[hh, rs, :]
                for j in range(tq // LANES):
                    cs = slice(j * LANES, (j + 1) * LANES)
                    if masked and j * LANES > (r + 1) * rc - 1:
                        continue
                    s = s_ref[hh, kb, rs, cs] - ck[:, cs]
                    if masked and (j + 1) * LANES - 1 > r * rc:
                        rid = r * rc + lax.broadcasted_iota(jnp.int32, (rc, LANES), 0)
                        cid = j * LANES + lax.broadcasted_iota(jnp.int32, (rc, LANES), 1)
                        s = jnp.where(rid >= cid, s, fill_ref[hh, rs, :])
                    if masked:
                        s_ref[hh, kb, rs, cs] = s
                    m = jnp.maximum(m, s)
                m_ref[hh, rs, :] = m

    def weights(kb, masked):
        k0 = pl.multiple_of(kb * tq, tq)
        v = v_ref[0, pl.ds(k0, tq), :].astype(BF16)
        for hh in heads:
            ck = key_bias(hh, kb)
            for r in range(tq // rc):
                rs = slice(r * rc, (r + 1) * rc)
                m = m_ref[hh, rs, :]
                lsum = l_ref[hh, rs, :]
                for j in range(tq // LANES):
                    cs = slice(j * LANES, (j + 1) * LANES)
                    if masked and j * LANES > (r + 1) * rc - 1:
                        p_ref[hh, rs, cs] = jnp.zeros((rc, LANES), BF16)
                        continue
                    s = s_ref[hh, kb, rs, cs]
                    if not masked:
                        s = s - ck[:, cs]
                    p = jnp.exp2(s - m)
                    lsum = lsum + p
                    p_ref[hh, rs, cs] = p.astype(BF16)
                l_ref[hh, rs, :] = lsum
            acc_ref[hh] += jnp.dot(p_ref[hh], v, preferred_element_type=F32)

    def loop(fn):
        def body(kb, carry):
            fn(kb, masked=False)
            return carry
        lax.fori_loop(0, qi, body, 0)
        fn(qi, masked=True)

    loop(scores)
    for hh in heads:
        m_ref[hh] = jnp.broadcast_to(jnp.max(m_ref[hh], axis=-1, keepdims=True), (tq, LANES))
    loop(weights)

    low = (lane // dh) == 0
    o = jnp.where(low, acc_ref[0] / jnp.sum(l_ref[0], axis=-1, keepdims=True),
                  acc_ref[1] / jnp.sum(l_ref[1], axis=-1, keepdims=True))
    sq = o * o
    ss0 = jnp.sum(jnp.where(low, sq, 0.0), axis=-1, keepdims=True)
    ss1 = jnp.sum(jnp.where(low, 0.0, sq), axis=-1, keepdims=True)
    inv = jnp.where(low, lax.rsqrt(ss0 / dh + EPS), lax.rsqrt(ss1 / dh + EPS))
    o_ref[0] = (o * inv * gain_ref[...]).astype(o_ref.dtype)


def _fox_prompt(q, k, v, cum, cumt, gain, tq=512):
    bn, t_len, d = q.shape
    tq = min(tq, t_len)
    nhp = d // LANES
    scale = float((LANES // 2) ** -0.5)
    return pl.pallas_call(
        functools.partial(_fox_prompt_kernel, scale=scale),
        grid=(bn, nhp, t_len // tq),
        in_specs=[
            pl.BlockSpec((1, tq, LANES), lambda b, h, i: (b, i, h)),
            pl.BlockSpec((1, t_len, LANES), lambda b, h, i: (b, 0, h)),
            pl.BlockSpec((1, t_len, LANES), lambda b, h, i: (b, 0, h)),
            pl.BlockSpec((1, tq, H_FOX), lambda b, h, i: (b, i, 0)),
            pl.BlockSpec((1, H_FOX, t_len), lambda b, h, i: (b, 0, 0)),
            pl.BlockSpec((1, LANES), lambda b, h, i: (0, h)),
        ],
        out_specs=pl.BlockSpec((1, tq, LANES), lambda b, h, i: (b, i, h)),
        out_shape=jax.ShapeDtypeStruct((bn, t_len, d), BF16),
        scratch_shapes=[pltpu.VMEM((2, t_len // tq, tq, tq), F32),
                        pltpu.VMEM((2, tq, tq), BF16),
                        pltpu.VMEM((2, tq, LANES), F32),
                        pltpu.VMEM((2, tq, LANES), F32),
                        pltpu.VMEM((2, tq, LANES), F32),
                        pltpu.VMEM((2, tq, LANES), F32)],
        compiler_params=_params("parallel", "parallel", "arbitrary"),
        name="fox_prompt",
    )(q, k, v, cum, cumt, gain.reshape(1, d))


def _suffix_kernel(lf_ref, tab_ref):
    tp, nh, n = lf_ref.shape
    i = lax.broadcasted_iota(jnp.int32, (n, n), 0)
    j = lax.broadcasted_iota(jnp.int32, (n, n), 1)
    lf = lf_ref[...].reshape(tp * nh, n)
    suf = _dot01_right(lf, (i > j).astype(BF16))
    tot = _dot01_right(lf, jnp.ones((n, n), BF16))
    tab_ref[:, :nh, :] = suf.reshape(tp, nh, n)
    tab_ref[:, nh:, :] = tot.reshape(tp, nh, n)


def _suffix(lf_t, tp=128):
    pages, nh, n = lf_t.shape
    tp = min(tp, pages)
    assert pages % tp == 0
    return pl.pallas_call(
        _suffix_kernel,
        grid=(pages // tp,),
        in_specs=[pl.BlockSpec((tp, nh, n), lambda i: (i, 0, 0))],
        out_specs=pl.BlockSpec((tp, 2 * nh, n), lambda i: (i, 0, 0)),
        out_shape=jax.ShapeDtypeStruct((pages, 2 * nh, n), F32),
        compiler_params=_params("parallel"),
        name="suffix",
    )(lf_t)


def _fox_decode_kernel(pt_ref, q_ref, kn_ref, vn_ref, cq_ref, gain_ref, tab_ref, *rest,
                       pps, scale):
    k_refs = rest[:pps]
    v_refs = rest[pps:2 * pps]
    o_ref, m_ref, l_ref, acc_ref, carry_ref, qbd_ref = rest[2 * pps:]
    n_pages = pl.num_programs(1) * pps
    b = pl.program_id(0)
    t_new = q_ref.shape[1]
    d = q_ref.shape[2]
    dh = d // H_FOX
    rows = t_new * H_FOX
    page = k_refs[0].shape[2]
    j = pl.program_id(1)
    head_of_lane = lax.broadcasted_iota(jnp.int32, (H_FOX, d), 1) // dh
    head_of_row = lax.broadcasted_iota(jnp.int32, (H_FOX, d), 0)
    diag = head_of_lane == head_of_row
    cq = cq_ref[0]

    @pl.when(j == 0)
    def _():
        qbd = jnp.concatenate(
            [jnp.where(diag, q_ref[0, t:t + 1, :] * scale, 0.0) for t in range(t_new)], axis=0)
        qbd_ref[...] = qbd
        qrow = lax.broadcasted_iota(jnp.int32, (rows, 1), 0) // H_FOX
        s_new = []
        for s in range(t_new):
            dot_s = jnp.sum(qbd * kn_ref[0, s:s + 1, :], axis=-1, keepdims=True)
            cs = jnp.concatenate([cq[s * H_FOX:(s + 1) * H_FOX]] * t_new, axis=0)
            s_new.append(jnp.where(qrow >= s, dot_s + cq - cs, NEG))
        m0 = s_new[0]
        for s in range(1, t_new):
            m0 = jnp.maximum(m0, s_new[s])
        l0 = jnp.zeros((rows, 1), F32)
        acc0 = jnp.zeros((rows, d), F32)
        for s in range(t_new):
            p_s = jnp.exp(s_new[s] - m0)
            l0 = l0 + p_s
            acc0 = acc0 + p_s * vn_ref[0, s:s + 1, :]
        m_ref[...] = jnp.broadcast_to(m0, (rows, LANES))
        l_ref[...] = jnp.broadcast_to(l0, (rows, LANES))
        acc_ref[...] = acc0
        carry_ref[...] = jnp.zeros_like(carry_ref)

    qbd = qbd_ref[...].astype(BF16)
    carry = carry_ref[...]
    groups = [range(g0, min(g0 + DECODE_GROUP, pps)) for g0 in range(0, pps, DECODE_GROUP)]
    scores = []
    for group in groups:
        s_parts = []
        for i in group:
            kpg = k_refs[i][0].astype(BF16)
            s = jnp.dot(qbd, kpg, preferred_element_type=F32)
            pid = pt_ref[b, n_pages - 1 - (j * pps + i)]
            suf = tab_ref[pid, :H_FOX, :] + carry
            carry = carry + tab_ref[pid, H_FOX:, :]
            s_parts.append(s + jnp.concatenate([suf] * t_new, axis=0) + cq)
        scores.append(jnp.concatenate(s_parts, axis=1))
    carry_ref[...] = carry
    softmaxes = []
    for s_g in scores:
        m_g = jnp.max(s_g, axis=-1, keepdims=True)
        p = jnp.exp(s_g - m_g)
        softmaxes.append((m_g, jnp.sum(p, axis=-1, keepdims=True), p.astype(BF16)))
    partials = []
    for group, (m_g, l_g, p_bf) in zip(groups, softmaxes):
        vt = jnp.concatenate([v_refs[i][0] for i in group], axis=1).astype(BF16)
        partials.append((m_g, l_g, lax.dot_general(p_bf, vt, NT_DIMS, preferred_element_type=F32)))
    m_prev = m_ref[...][:, :1]
    m_new = m_prev
    for m_g, _, _ in partials:
        m_new = jnp.maximum(m_new, m_g)
    alpha = jnp.exp(m_prev - m_new)
    l_new = alpha * l_ref[...][:, :1]
    acc = alpha * acc_ref[...]
    for m_g, l_g, acc_g in partials:
        w_g = jnp.exp(m_g - m_new)
        l_new = l_new + w_g * l_g
        acc = acc + w_g * acc_g
    m_ref[...] = jnp.broadcast_to(m_new, m_ref.shape)
    l_ref[...] = jnp.broadcast_to(l_new, l_ref.shape)
    acc_ref[...] = acc

    @pl.when(j == pl.num_programs(1) - 1)
    def _():
        o = acc_ref[...] / l_ref[...][:, :1]
        o = jnp.where(jnp.concatenate([diag] * t_new, axis=0), o, 0.0)
        y = o * lax.rsqrt(jnp.sum(o * o, axis=-1, keepdims=True) / dh + EPS)
        out = jnp.concatenate(
            [jnp.sum(y[t * H_FOX:(t + 1) * H_FOX], axis=0, keepdims=True) for t in range(t_new)],
            axis=0)
        o_ref[0] = out * gain_ref[...]


def _fox_decode(q, k_new, v_new, cq_col, gain, cache_k, cache_v, suffix_tab, page_table,
                layer, n_pool, pps=8):
    bn, t_new, d = q.shape
    page = cache_k.shape[2]
    n_pages = page_table.shape[1]
    pps = min(pps, n_pages)
    assert n_pages % pps == 0
    rows = t_new * H_FOX
    scale = float((d // H_FOX) ** -0.5)
    base = layer * n_pool

    def page_idx(i):
        return lambda b, j, pt: (base + pt[b, n_pages - 1 - (j * pps + i)], 0, 0)

    tok = pl.BlockSpec((1, t_new, d), lambda b, j, pt: (b, 0, 0))
    in_specs = [tok, tok, tok,
                pl.BlockSpec((1, rows, 1), lambda b, j, pt: (b, 0, 0)),
                pl.BlockSpec((1, d), lambda b, j, pt: (0, 0)),
                _resident((None, n_pool, 2 * H_FOX, page), lambda b, j, pt: (layer, 0, 0, 0))]
    in_specs += [pl.BlockSpec((1, d, page), page_idx(i)) for i in range(pps)]
    in_specs += [pl.BlockSpec((1, d, page), page_idx(i)) for i in range(pps)]
    grid_spec = pltpu.PrefetchScalarGridSpec(
        num_scalar_prefetch=1,
        grid=(bn, n_pages // pps),
        in_specs=in_specs,
        out_specs=pl.BlockSpec((1, t_new, d), lambda b, j, pt: (b, 0, 0)),
        scratch_shapes=[pltpu.VMEM((rows, LANES), F32),
                        pltpu.VMEM((rows, LANES), F32),
                        pltpu.VMEM((rows, d), F32),
                        pltpu.VMEM((H_FOX, page), F32),
                        pltpu.VMEM((rows, d), F32)],
    )
    return pl.pallas_call(
        functools.partial(_fox_decode_kernel, pps=pps, scale=scale),
        grid_spec=grid_spec,
        out_shape=jax.ShapeDtypeStruct((bn, t_new, d), F32),
        compiler_params=_params("parallel", "arbitrary"),
        name="fox_decode",
    )(page_table, q, k_new, v_new, cq_col, gain.reshape(1, d), suffix_tab,
      *([cache_k] * pps), *([cache_v] * pps))


def kernel(x_prompt, x_sample, cache_k, cache_v, cache_logf, state_hgrn, page_table, norm_ffn1, ffn1_w_in, ffn1_w_out, norm_mix, w_in_mix, hgrn_lb, fox_f_bias, hgrn_out_gain, fox_out_gain, w_out_mix, norm_ffn2, ffn2_w_in, ffn2_w_out, norm_final):
    depth = norm_ffn1.shape[0]
    bn, t_len, d = x_prompt.shape
    db, t_new, _ = x_sample.shape
    d_fox = fox_out_gain.shape[1]
    d_h = hgrn_out_gain.shape[1]
    dh_fox = d_fox // H_FOX
    n_pool, page = cache_k.shape[1], cache_k.shape[2]
    d_main = 4 * d_h + 3 * d_fox

    w1_in, w1_out = ffn1_w_in.astype(BF16), ffn1_w_out.astype(BF16)
    w2_in, w2_out = ffn2_w_in.astype(BF16), ffn2_w_out.astype(BF16)
    w_mix = w_in_mix.astype(BF16)
    w_f = jnp.pad(w_in_mix[:, :, d_main:], ((0, 0), (0, 0), (0, LANES - H_FOX))).astype(BF16)
    f_bias = jnp.pad(fox_f_bias, ((0, 0), (0, LANES - H_FOX))).reshape(depth, 1, LANES)
    w_out = w_out_mix.astype(BF16)

    ck = jnp.transpose(cache_k, (0, 1, 3, 4, 2)).reshape(depth * n_pool, d_fox, page)
    cv = jnp.transpose(cache_v, (0, 1, 3, 4, 2)).reshape(depth * n_pool, d_fox, page)
    lf_t = jnp.swapaxes(cache_logf, 2, 3).reshape(depth * n_pool, H_FOX, page)
    suffix_tab = _suffix(lf_t).reshape(depth, n_pool, 2 * H_FOX, page)

    bf16_rows = 2 * SUBLANES
    t_pad = -(-t_new // bf16_rows) * bf16_rows

    def run(x, seq_len, nseq, prompt):
        ks, vs, lfs, ss = [], [], [], []
        for l in range(depth):
            x = _ffn(x, norm_ffn1[l], w1_in, w1_out, l)
            hg, q, k, v, lf, cum, cumt, *kv_t = _inproj(x, norm_mix[l], w_mix, w_f, f_bias, l, seq_len,
                                                        4 * d_h, d_fox)
            if kv_t:
                k_out, v_out = (a.reshape(nseq, H_FOX, dh_fox, seq_len).transpose(0, 3, 1, 2) for a in kv_t)
            else:
                k_out, v_out = (a.reshape(nseq, seq_len, H_FOX, dh_fox) for a in (k, v))
            if prompt:
                o_a, s_fin = _hgrn(hg.reshape(nseq, seq_len, 4 * d_h), hgrn_lb, hgrn_out_gain[l], None, l,
                                   tb=min(512, seq_len), chunk=64, sub=16, valid_len=None, out_dtype=BF16)
                o_a = o_a.reshape(nseq * seq_len, d_h)
                tiles = cumt.shape[0] // nseq
                cumt_b = cumt.reshape(nseq, tiles, H_FOX, -1).transpose(0, 2, 1, 3).reshape(nseq, H_FOX, seq_len)
                o_f = _fox_prompt(q.reshape(nseq, seq_len, d_fox), k.reshape(nseq, seq_len, d_fox),
                                  v.reshape(nseq, seq_len, d_fox), cum.reshape(nseq, seq_len, H_FOX),
                                  cumt_b, fox_out_gain[l]).reshape(nseq * seq_len, d_fox)
            else:
                hg_p = jnp.pad(hg.reshape(nseq, seq_len, 4 * d_h), ((0, 0), (0, t_pad - seq_len), (0, 0)))
                o_a, s_fin = _hgrn(hg_p, hgrn_lb, hgrn_out_gain[l], state_hgrn, l,
                                   tb=t_pad, chunk=t_pad, sub=t_pad, valid_len=seq_len, out_dtype=F32)
                o_a = o_a[:, :seq_len].reshape(nseq * seq_len, d_h).astype(BF16)
                o_f = _fox_decode(q.reshape(nseq, seq_len, d_fox), k.reshape(nseq, seq_len, d_fox),
                                  v.reshape(nseq, seq_len, d_fox), cum.reshape(nseq, seq_len * H_FOX, 1),
                                  fox_out_gain[l], ck, cv, suffix_tab, page_table, l, n_pool)
                o_f = o_f.reshape(nseq * seq_len, d_fox).astype(BF16)
            x = _ffn(x, norm_ffn2[l], w2_in, w2_out, l, mixer=(o_a, o_f, w_out),
                     final_g=norm_final if l == depth - 1 else None)
            ks.append(k_out)
            vs.append(v_out)
            lfs.append(lf.reshape(nseq, seq_len, H_FOX))
            ss.append(s_fin)
        return (x.reshape(nseq, seq_len, d), jnp.stack(ks), jnp.stack(vs), jnp.stack(lfs), jnp.stack(ss))

    y_p, k_p, v_p, lf_p, s_p = run(x_prompt.reshape(bn * t_len, d), t_len, bn, True)
    y_s, k_s, v_s, lf_s, s_s = run(x_sample.reshape(db * t_new, d), t_new, db, False)
    return (y_p, y_s, k_p, v_p, lf_p, s_p, k_s, v_s, lf_s, s_s)
```

```python
import functools

import jax
import jax.numpy as jnp
from jax import lax
from jax.experimental import pallas as pl
from jax.experimental.pallas import tpu as pltpu

F32 = jnp.float32
BF16 = jnp.bfloat16

EPS = 1e-6
TINY = 1e-30
NEG = -1e30

LOG2E = 1.4426950408889634
H_HGRN = 4
H_FOX = 8
FOX_ROW_CHUNK = 64
CUM_BLOCK = 256
DECODE_GROUP = 2
DECODE_SLOTS = 3
LANES = 128
SUBLANES = 8
VMEM_LIMIT = 56 * 1024 * 1024

NT_DIMS = (((1,), (1,)), ((), ()))
TN_DIMS = (((0,), (0,)), ((), ()))


def _params(*sem):
    return pltpu.CompilerParams(dimension_semantics=sem, vmem_limit_bytes=VMEM_LIMIT)


def _resident(shape, index_map):
    return pl.BlockSpec(shape, index_map, pipeline_mode=pl.Buffered(1))


def _rms(x, g):
    return x * lax.rsqrt(jnp.mean(x * x, axis=-1, keepdims=True) + EPS) * g


def _split3(x):
    hi = x.astype(BF16)
    r = x - hi.astype(F32)
    mid = r.astype(BF16)
    lo = (r - mid.astype(F32)).astype(BF16)
    return hi, mid, lo


def _dot01_left(m01, x):
    return sum(jnp.dot(m01, t, preferred_element_type=F32) for t in _split3(x))


def _dot01_right(x, m01):
    return sum(jnp.dot(t, m01, preferred_element_type=F32) for t in _split3(x))


def _ffn_kernel(x_ref, g_ref, wa_ref, wb_ref, wo_ref, *rest, final_norm, mixer_out):
    rest = list(rest)
    o_ref = rest.pop()
    x = x_ref[...]
    if mixer_out:
        oa_ref, of_ref, wma_ref, wmf_ref = rest[:4]
        del rest[:4]
        x = (x + jnp.dot(oa_ref[...], wma_ref[...], preferred_element_type=F32)
             + jnp.dot(of_ref[...], wmf_ref[...], preferred_element_type=F32))
    if final_norm:
        (gf_ref,) = rest
    xn = _rms(x, g_ref[...]).astype(BF16)
    a = jnp.dot(xn, wa_ref[...], preferred_element_type=F32)
    b = jnp.dot(xn, wb_ref[...], preferred_element_type=F32)
    h = (a * jax.nn.sigmoid(a) * b).astype(BF16)
    y = x + 0.5 * jnp.dot(h, wo_ref[...], preferred_element_type=F32)
    if final_norm:
        y = _rms(y, gf_ref[...])
    o_ref[...] = y


def _ffn(x, g, w_in, w_out, layer, mixer=None, final_g=None, tm=512):
    n, d = x.shape
    ff = w_out.shape[1]
    tm = min(tm, n)
    in_specs = [
        pl.BlockSpec((tm, d), lambda i: (i, 0)),
        _resident((1, d), lambda i: (0, 0)),
        _resident((None, d, ff), lambda i: (layer, 0, 0)),
        _resident((None, d, ff), lambda i: (layer, 0, 1)),
        _resident((None, ff, d), lambda i: (layer, 0, 0)),
    ]
    args = [x, g.reshape(1, d), w_in, w_in, w_out]
    if mixer is not None:
        o_a, o_f, w_mix_out = mixer
        dm = o_a.shape[1]
        assert o_f.shape[1] == dm and w_mix_out.shape[1] == 2 * dm
        in_specs += [pl.BlockSpec((tm, dm), lambda i: (i, 0)),
                     pl.BlockSpec((tm, dm), lambda i: (i, 0)),
                     _resident((None, dm, d), lambda i: (layer, 0, 0)),
                     _resident((None, dm, d), lambda i: (layer, 1, 0))]
        args += [o_a, o_f, w_mix_out, w_mix_out]
    if final_g is not None:
        in_specs.append(_resident((1, d), lambda i: (0, 0)))
        args.append(final_g.reshape(1, d))
    return pl.pallas_call(
        functools.partial(_ffn_kernel, final_norm=final_g is not None, mixer_out=mixer is not None),
        grid=(n // tm,),
        in_specs=in_specs,
        out_specs=pl.BlockSpec((tm, d), lambda i: (i, 0)),
        out_shape=jax.ShapeDtypeStruct((n, d), F32),
        compiler_params=_params("parallel"),
        name="ffn",
    )(*args)


def _inproj_kernel(x_ref, g_ref, w_ref, wf_ref, fb_ref, *rest,
                   seq_len, d_hg, d_fox, feature_major, n_earlier):
    rest = list(rest)
    carry_ref = rest.pop()
    if n_earlier:
        pk_ref, pv_ref = rest[:2]
        del rest[:2]
    if feature_major:
        hg_ref, q_ref, v_ref, lf_ref, cum_ref, cumt_ref, kt_ref, vt_ref = rest
    else:
        hg_ref, q_ref, k_ref, v_ref, lf_ref, cum_ref, cumt_ref = rest
    tm = x_ref.shape[0]
    i = pl.program_id(0)
    xn = _rms(x_ref[...], g_ref[...]).astype(BF16)
    p = jnp.dot(xn, w_ref[...], preferred_element_type=F32)
    hg_ref[...] = p[:, :d_hg]
    q_ref[...] = p[:, d_hg:d_hg + d_fox]
    k = p[:, d_hg + d_fox:d_hg + 2 * d_fox]
    v = p[:, d_hg + 2 * d_fox:d_hg + 3 * d_fox]
    v_ref[...] = v
    if feature_major:
        if n_earlier:
            kt_ref[:n_earlier, 0] = pk_ref[:, 0]
            vt_ref[:n_earlier, 0] = pv_ref[:, 0]
        kt_ref[n_earlier, 0] = k.T
        vt_ref[n_earlier, 0] = v.T
    else:
        k_ref[...] = k

    z = jnp.dot(xn, wf_ref[...], preferred_element_type=F32) + fb_ref[...]
    lf = jnp.minimum(z, 0.0) - jnp.log1p(jnp.exp(-jnp.abs(z)))

    bs = min(CUM_BLOCK, tm)
    assert seq_len >= tm or bs % seq_len == 0
    row = lax.broadcasted_iota(jnp.int32, (bs, bs), 0)
    col = lax.broadcasted_iota(jnp.int32, (bs, bs), 1)
    same = col <= row
    if seq_len < tm:
        same = same & ((row // seq_len) == (col // seq_len))
    same = same.astype(BF16)
    offset = jnp.zeros((1, LANES), F32)
    if seq_len > tm:
        tiles_per_seq = seq_len // tm

        @pl.when(i % tiles_per_seq == 0)
        def _():
            carry_ref[...] = jnp.zeros_like(carry_ref)

        offset = carry_ref[...]
    blocks = []
    for r0 in range(0, tm, bs):
        blocks.append(_dot01_left(same, lf[r0:r0 + bs]) + offset)
        if seq_len >= tm:
            offset = blocks[-1][bs - 1:bs, :]
    cum = jnp.concatenate(blocks, axis=0) if len(blocks) > 1 else blocks[0]
    if seq_len > tm:
        carry_ref[...] = offset
    lf_ref[...] = lf[:, :H_FOX]
    cum_ref[...] = cum[:, :H_FOX]
    cumt_ref[0] = cum.T[:H_FOX, :]


def _inproj(x, g, w_all, w_f, f_bias, layer, seq_len, d_hg, d_fox, earlier=None, tm=512):
    n, d = x.shape
    tm = min(tm, n)
    assert seq_len % tm == 0 or tm % seq_len == 0
    d_main = d_hg + 3 * d_fox
    nt = n // tm
    tok = lambda w: pl.BlockSpec((tm, w), lambda i: (i, 0))
    tok_shape = lambda w: jax.ShapeDtypeStruct((n, w), F32)
    feature_major = seq_len % tm == 0
    n_earlier = 0 if earlier is None else earlier[0].shape[0]
    in_specs = [
        tok(d),
        _resident((1, d), lambda i: (0, 0)),
        _resident((None, d, d_main), lambda i: (layer, 0, 0)),
        _resident((None, d, LANES), lambda i: (layer, 0, 0)),
        _resident((None, 1, LANES), lambda i: (layer, 0, 0)),
    ]
    args = [x, g.reshape(1, d), w_all, w_f, f_bias]
    tail_specs = [tok(H_FOX), tok(H_FOX), pl.BlockSpec((1, H_FOX, tm), lambda i: (i, 0, 0))]
    tail_shape = [tok_shape(H_FOX), tok_shape(H_FOX), jax.ShapeDtypeStruct((nt, H_FOX, tm), F32)]
    if feature_major:
        tps = seq_len // tm
        stack = lambda layers: pl.BlockSpec((layers, 1, d_fox, tm), lambda i: (0, i // tps, 0, i % tps))
        if n_earlier:
            in_specs += [stack(n_earlier)] * 2
            args += list(earlier)
        out_specs = [tok(d_hg), tok(d_fox), tok(d_fox)] + tail_specs + [stack(n_earlier + 1)] * 2
        out_shape = ([tok_shape(d_hg), tok_shape(d_fox), tok_shape(d_fox)] + tail_shape
                     + [jax.ShapeDtypeStruct((n_earlier + 1, n // seq_len, d_fox, seq_len), F32)] * 2)
    else:
        assert earlier is None
        out_specs = [tok(d_hg), tok(d_fox), tok(d_fox), tok(d_fox)] + tail_specs
        out_shape = [tok_shape(d_hg), tok_shape(d_fox), tok_shape(d_fox), tok_shape(d_fox)] + tail_shape
    return pl.pallas_call(
        functools.partial(_inproj_kernel, seq_len=seq_len, d_hg=d_hg, d_fox=d_fox,
                          feature_major=feature_major, n_earlier=n_earlier),
        grid=(nt,),
        in_specs=in_specs,
        out_specs=out_specs,
        out_shape=out_shape,
        scratch_shapes=[pltpu.VMEM((1, LANES), F32)],
        compiler_params=_params("arbitrary"),
        name="inproj",
    )(*args)


def _hgrn_kernel(lbp_ref, aq_ref, af_ref, ai_ref, ag_ref, gain_ref, *rest,
                 layer, chunk, sub, valid_len, zero_init):
    if zero_init:
        o_ref, sout_ref, st_ref, ks_ref, as_ref = rest
    else:
        s0_ref, o_ref, sout_ref, st_ref, ks_ref, as_ref = rest
    tb = aq_ref.shape[1]
    nh = aq_ref.shape[2] // LANES
    nchunks = tb // chunk
    t = pl.program_id(1)

    @pl.when(t == 0)
    def _():
        for h in range(nh):
            if zero_init:
                st_ref[h] = jnp.zeros((LANES, LANES), F32)
            else:
                st_ref[h] = s0_ref[0, 0, h].T

    lbp = lbp_ref[...]
    e = jnp.exp(lbp - jnp.max(lbp, axis=0, keepdims=True))
    prob = e / jnp.sum(e, axis=0, keepdims=True)
    lb = jnp.zeros((1, nh * LANES), F32)
    for j in range(1, layer + 1):
        lb = lb + prob[j:j + 1, :]

    nsub = chunk // sub
    row = lax.broadcasted_iota(jnp.int32, (chunk, 1), 0)
    rmod = row % sub
    lane_off = lax.broadcasted_iota(jnp.int32, (chunk, LANES), 1) - (row - rmod)
    tri = (lax.broadcasted_iota(jnp.int32, (chunk, chunk), 1)
           <= lax.broadcasted_iota(jnp.int32, (chunk, chunk), 0)).astype(BF16)
    zpad_bf = jnp.zeros((LANES - chunk, LANES), BF16)
    gain_all = gain_ref[...]

    def chunk_body(c, carry):
        r0 = pl.multiple_of(c * chunk, chunk)
        rows = pl.ds(r0, chunk)
        f_all = lb + (1.0 - lb) * jax.nn.sigmoid(af_ref[0, rows, :])
        logf_all = jnp.log(jnp.maximum(f_all, TINY))
        k_all = 1.0 - f_all
        if valid_len is not None:
            live = (t * tb + r0 + row) < valid_len
            logf_all = jnp.where(live, logf_all, 0.0)
            k_all = jnp.where(live, k_all, 0.0)
        a_all = _dot01_left(tri, logf_all) * LOG2E
        ks_ref[...] = k_all
        as_ref[...] = a_all

        for h in range(nh):
            hs = slice(h * LANES, (h + 1) * LANES)
            q = aq_ref[0, rows, hs]
            v = ai_ref[0, rows, hs]
            g = ag_ref[0, rows, hs]
            k, a = k_all[:, hs], a_all[:, hs]
            st = st_ref[h]
            v_bf = v.astype(BF16)

            o = lax.dot_general((q * jnp.exp2(a)).astype(BF16), st.astype(BF16), NT_DIMS,
                                preferred_element_type=F32)

            blocks = [jnp.zeros((sub, LANES), F32)]
            for i in range(1, nsub):
                lo, hi = i * sub, (i + 1) * sub
                r = a[lo - 1:lo, :]
                qt = (q[lo:hi] * jnp.exp2(a[lo:hi] - r)).astype(BF16)
                kt = jnp.where(row < lo, k * jnp.exp2(r - a), 0.0).astype(BF16)
                blocks.append(lax.dot_general(qt, jnp.concatenate([kt, zpad_bf], axis=0), NT_DIMS,
                                              preferred_element_type=F32))
            sc = jnp.concatenate(blocks, axis=0) if nsub > 1 else blocks[0]
            for j in range(sub):
                kj = jnp.concatenate([jnp.broadcast_to(ks_ref[i * sub + j:i * sub + j + 1, hs], (sub, LANES))
                                      for i in range(nsub)], axis=0)
                aj = jnp.concatenate([jnp.broadcast_to(as_ref[i * sub + j:i * sub + j + 1, hs], (sub, LANES))
                                      for i in range(nsub)], axis=0)
                w = q * kj * jnp.exp2(a - aj)
                col = jnp.where(rmod >= j, jnp.sum(w, axis=-1, keepdims=True), 0.0)
                sc = jnp.where(lane_off == j, col, sc)
            o = o + jnp.dot(sc.astype(BF16), jnp.concatenate([v_bf, zpad_bf], axis=0),
                            preferred_element_type=F32)

            a_last = a[chunk - 1:chunk, :]
            kt = (k * jnp.exp2(a_last - a)).astype(BF16)
            st_ref[h] = st * jnp.exp2(a_last) + lax.dot_general(
                v_bf, kt, TN_DIMS, preferred_element_type=F32)

            y = o * lax.rsqrt(jnp.mean(o * o, axis=-1, keepdims=True) + EPS) * gain_all[:, hs]
            y = y * (g * jax.nn.sigmoid(g))
            o_ref[0, rows, hs] = y.astype(o_ref.dtype)
        return carry

    lax.fori_loop(0, nchunks, chunk_body, 0)

    @pl.when(t == pl.num_programs(1) - 1)
    def _():
        for h in range(nh):
            sout_ref[0, h] = st_ref[h].T


def _hgrn(hg, lb_param, gain, state0, layer, *, tb, chunk, sub, valid_len, out_dtype):
    bn, t_len, d4 = hg.shape
    dh = d4 // 4
    nh = dh // LANES
    depth = lb_param.shape[0]
    blk = lambda c: pl.BlockSpec((1, tb, dh), lambda b, t: (b, t, c))
    in_specs = [pl.BlockSpec((depth, dh), lambda b, t: (0, 0)),
                blk(0), blk(1), blk(2), blk(3),
                pl.BlockSpec((1, dh), lambda b, t: (0, 0))]
    args = [lb_param, hg, hg, hg, hg, gain.reshape(1, dh)]
    if state0 is not None:
        in_specs.append(pl.BlockSpec((1, 1, nh, LANES, LANES), lambda b, t: (layer, b, 0, 0, 0)))
        args.append(state0)
    return pl.pallas_call(
        functools.partial(_hgrn_kernel, layer=layer, chunk=chunk, sub=sub,
                          valid_len=valid_len, zero_init=state0 is None),
        grid=(bn, t_len // tb),
        in_specs=in_specs,
        out_specs=[pl.BlockSpec((1, tb, dh), lambda b, t: (b, t, 0)),
                   pl.BlockSpec((1, nh, LANES, LANES), lambda b, t: (b, 0, 0, 0))],
        out_shape=[jax.ShapeDtypeStruct((bn, t_len, dh), out_dtype),
                   jax.ShapeDtypeStruct((bn, nh, LANES, LANES), F32)],
        scratch_shapes=[pltpu.VMEM((nh, LANES, LANES), F32),
                        pltpu.VMEM((chunk, dh), F32),
                        pltpu.VMEM((chunk, dh), F32)],
        compiler_params=_params("parallel", "arbitrary"),
        name="hgrn",
    )(*args)


def _fox_prompt_kernel(q_ref, kt_ref, v_ref, cum_ref, cumt_ref, gain_ref, o_ref,
                       s_ref, p_ref, fill_ref, m_ref, l_ref, acc_ref, *, scale):
    tq = q_ref.shape[1]
    rc = FOX_ROW_CHUNK
    dh = LANES // 2
    hp = pl.program_id(1)
    qi = pl.program_id(2)
    lane = lax.broadcasted_iota(jnp.int32, (1, LANES), 1)
    q = q_ref[0] * (scale * LOG2E)
    cum = cum_ref[0]
    hl = lax.broadcasted_iota(jnp.int32, (1, H_FOX), 1)

    heads = range(2)
    qms = []
    for hh in heads:
        qms.append(jnp.where((lane // dh) == hh, q, 0.0).astype(BF16))
        cq = jnp.sum(jnp.where(hl == 2 * hp + hh, cum, 0.0), axis=-1, keepdims=True)
        fill_ref[hh] = jnp.broadcast_to((NEG - cq) * LOG2E, (tq, LANES))
        m_ref[hh] = jnp.full((tq, LANES), -jnp.inf, F32)
        l_ref[hh] = jnp.zeros((tq, LANES), F32)
        acc_ref[hh] = jnp.zeros((tq, LANES), F32)

    def key_bias(hh, kb):
        k0 = pl.multiple_of(kb * tq, tq)
        return cumt_ref[0, pl.ds(2 * hp + hh, 1), pl.ds(k0, tq)] * LOG2E

    def scores(kb, masked):
        k0 = pl.multiple_of(kb * tq, tq)
        kt = kt_ref[0, :, pl.ds(k0, tq)].astype(BF16)
        for hh in heads:
            s_ref[hh, kb] = jnp.dot(qms[hh], kt, preferred_element_type=F32)
        for hh in heads:
            ck = key_bias(hh, kb)
            for r in range(tq // rc):
                rs = slice(r * rc, (r + 1) * rc)
                m = m_ref[hh, rs, :]
                for j in range(tq // LANES):
                    cs = slice(j * LANES, (j + 1) * LANES)
                    if masked and j * LANES > (r + 1) * rc - 1:
                        continue
                    s = s_ref[hh, kb, rs, cs] - ck[:, cs]
                    if masked and (j + 1) * LANES - 1 > r * rc:
                        rid = r * rc + lax.broadcasted_iota(jnp.int32, (rc, LANES), 0)
                        cid = j * LANES + lax.broadcasted_iota(jnp.int32, (rc, LANES), 1)
                        s = jnp.where(rid >= cid, s, fill_ref[hh, rs, :])
                    if masked:
                        s_ref[hh, kb, rs, cs] = s
                    m = jnp.maximum(m, s)
                m_ref[hh, rs, :] = m

    def weights(kb, masked):
        k0 = pl.multiple_of(kb * tq, tq)
        v = v_ref[0, pl.ds(k0, tq), :].astype(BF16)
        for hh in heads:
            ck = key_bias(hh, kb)
            for r in range(tq // rc):
                rs = slice(r * rc, (r + 1) * rc)
                m = m_ref[hh, rs, :]
                lsum = l_ref[hh, rs, :]
                for j in range(tq // LANES):
                    cs = slice(j * LANES, (j + 1) * LANES)
                    if masked and j * LANES > (r + 1) * rc - 1:
                        p_ref[hh, rs, cs] = jnp.zeros((rc, LANES), BF16)
                        continue
                    s = s_ref[hh, kb, rs, cs]
                    if not masked:
                        s = s - ck[:, cs]
                    p = jnp.exp2(s - m)
                    lsum = lsum + p
                    p_ref[hh, rs, cs] = p.astype(BF16)
                l_ref[hh, rs, :] = lsum
            acc_ref[hh] += jnp.dot(p_ref[hh], v, preferred_element_type=F32)

    def loop(fn):
        def body(kb, carry):
            fn(kb, masked=False)
            return carry
        lax.fori_loop(0, qi, body, 0)
        fn(qi, masked=True)

    loop(scores)
    for hh in heads:
        m_ref[hh] = jnp.broadcast_to(jnp.max(m_ref[hh], axis=-1, keepdims=True), (tq, LANES))
    loop(weights)

    low = (lane // dh) == 0
    o = jnp.where(low, acc_ref[0] / jnp.sum(l_ref[0], axis=-1, keepdims=True),
                  acc_ref[1] / jnp.sum(l_ref[1], axis=-1, keepdims=True))
    sq = o * o
    ss0 = jnp.sum(jnp.where(low, sq, 0.0), axis=-1, keepdims=True)
    ss1 = jnp.sum(jnp.where(low, 0.0, sq), axis=-1, keepdims=True)
    inv = jnp.where(low, lax.rsqrt(ss0 / dh + EPS), lax.rsqrt(ss1 / dh + EPS))
    o_ref[0] = (o * inv * gain_ref[...]).astype(o_ref.dtype)


def _fox_prompt(q, kt_stack, layer, v, cum, cumt, gain, tq=512):
    bn, t_len, d = q.shape
    tq = min(tq, t_len)
    nhp = d // LANES
    scale = float((LANES // 2) ** -0.5)
    return pl.pallas_call(
        functools.partial(_fox_prompt_kernel, scale=scale),
        grid=(bn, nhp, t_len // tq),
        in_specs=[
            pl.BlockSpec((1, tq, LANES), lambda b, h, i: (b, i, h)),
            pl.BlockSpec((None, 1, LANES, t_len), lambda b, h, i: (layer, b, h, 0)),
            pl.BlockSpec((1, t_len, LANES), lambda b, h, i: (b, 0, h)),
            pl.BlockSpec((1, tq, H_FOX), lambda b, h, i: (b, i, 0)),
            pl.BlockSpec((1, H_FOX, t_len), lambda b, h, i: (b, 0, 0)),
            pl.BlockSpec((1, LANES), lambda b, h, i: (0, h)),
        ],
        out_specs=pl.BlockSpec((1, tq, LANES), lambda b, h, i: (b, i, h)),
        out_shape=jax.ShapeDtypeStruct((bn, t_len, d), BF16),
        scratch_shapes=[pltpu.VMEM((2, t_len // tq, tq, tq), F32),
                        pltpu.VMEM((2, tq, tq), BF16),
                        pltpu.VMEM((2, tq, LANES), F32),
                        pltpu.VMEM((2, tq, LANES), F32),
                        pltpu.VMEM((2, tq, LANES), F32),
                        pltpu.VMEM((2, tq, LANES), F32)],
        compiler_params=_params("parallel", "parallel", "arbitrary"),
        name="fox_prompt",
    )(q, kt_stack, v, cum, cumt, gain.reshape(1, d))


def _suffix_kernel(lf_ref, tab_ref):
    tp, nh, n = lf_ref.shape
    i = lax.broadcasted_iota(jnp.int32, (n, n), 0)
    j = lax.broadcasted_iota(jnp.int32, (n, n), 1)
    lf = lf_ref[...].reshape(tp * nh, n)
    suf = _dot01_right(lf, (i > j).astype(BF16))
    tot = _dot01_right(lf, jnp.ones((n, n), BF16))
    tab_ref[:, :nh, :] = suf.reshape(tp, nh, n)
    tab_ref[:, nh:, :] = tot.reshape(tp, nh, n)


def _suffix(lf_t, tp=128):
    pages, nh, n = lf_t.shape
    tp = min(tp, pages)
    assert pages % tp == 0
    return pl.pallas_call(
        _suffix_kernel,
        grid=(pages // tp,),
        in_specs=[pl.BlockSpec((tp, nh, n), lambda i: (i, 0, 0))],
        out_specs=pl.BlockSpec((tp, 2 * nh, n), lambda i: (i, 0, 0)),
        out_shape=jax.ShapeDtypeStruct((pages, 2 * nh, n), F32),
        compiler_params=_params("parallel"),
        name="suffix",
    )(lf_t)


def _fox_decode_kernel(pt_ref, q_ref, kn_ref, vn_ref, cq_ref, gain_ref, tab_ref, ck_hbm, cv_hbm,
                       o_ref, m_ref, l_ref, acc_ref, carry_ref, qbd_ref, kbuf, vbuf, sem,
                       *, pps, scale, first_page):
    steps = pl.num_programs(1)
    n_pages = steps * pps
    b = pl.program_id(0)
    j = pl.program_id(1)
    t_new = q_ref.shape[1]
    d = q_ref.shape[2]
    dh = d // H_FOX
    rows = t_new * H_FOX
    page = kbuf.shape[3]

    g = b * steps + j
    total = pl.num_programs(0) * steps

    def page_copies(gg, slot):
        bb = gg // steps
        jj = gg - bb * steps
        copies = []
        for i in range(pps):
            pid = first_page + pt_ref[bb, n_pages - 1 - (jj * pps + i)]
            copies.append(pltpu.make_async_copy(ck_hbm.at[pid], kbuf.at[slot, i], sem.at[slot, 0, i]))
            copies.append(pltpu.make_async_copy(cv_hbm.at[pid], vbuf.at[slot, i], sem.at[slot, 1, i]))
        return copies

    @pl.when(g == 0)
    def _():
        for ahead in range(DECODE_SLOTS - 1):
            @pl.when(ahead < total)
            def _():
                for c in page_copies(ahead, ahead):
                    c.start()

    @pl.when(g + DECODE_SLOTS - 1 < total)
    def _():
        nxt = g + DECODE_SLOTS - 1
        for c in page_copies(nxt, nxt % DECODE_SLOTS):
            c.start()

    slot = g % DECODE_SLOTS
    for c in page_copies(g, slot):
        c.wait()
    k_refs = [kbuf.at[slot, i] for i in range(pps)]
    v_refs = [vbuf.at[slot, i] for i in range(pps)]
    head_of_lane = lax.broadcasted_iota(jnp.int32, (H_FOX, d), 1) // dh
    head_of_row = lax.broadcasted_iota(jnp.int32, (H_FOX, d), 0)
    diag = head_of_lane == head_of_row
    cq = cq_ref[0]

    @pl.when(j == 0)
    def _():
        qbd = jnp.concatenate(
            [jnp.where(diag, q_ref[0, t:t + 1, :] * scale, 0.0) for t in range(t_new)], axis=0)
        qbd_ref[...] = qbd
        qrow = lax.broadcasted_iota(jnp.int32, (rows, 1), 0) // H_FOX
        s_new = []
        for s in range(t_new):
            dot_s = jnp.sum(qbd * kn_ref[0, s:s + 1, :], axis=-1, keepdims=True)
            cs = jnp.concatenate([cq[s * H_FOX:(s + 1) * H_FOX]] * t_new, axis=0)
            s_new.append(jnp.where(qrow >= s, dot_s + cq - cs, NEG))
        m0 = s_new[0]
        for s in range(1, t_new):
            m0 = jnp.maximum(m0, s_new[s])
        l0 = jnp.zeros((rows, 1), F32)
        acc0 = jnp.zeros((rows, d), F32)
        for s in range(t_new):
            p_s = jnp.exp(s_new[s] - m0)
            l0 = l0 + p_s
            acc0 = acc0 + p_s * vn_ref[0, s:s + 1, :]
        m_ref[...] = jnp.broadcast_to(m0, (rows, LANES))
        l_ref[...] = jnp.broadcast_to(l0, (rows, LANES))
        acc_ref[...] = acc0
        carry_ref[...] = jnp.zeros_like(carry_ref)

    qbd = qbd_ref[...].astype(BF16)
    carry = carry_ref[...]
    groups = [range(g0, min(g0 + DECODE_GROUP, pps)) for g0 in range(0, pps, DECODE_GROUP)]
    scores = []
    for group in groups:
        s_parts = []
        for i in group:
            kpg = k_refs[i][...].astype(BF16)
            s = jnp.dot(qbd, kpg, preferred_element_type=F32)
            pid = pt_ref[b, n_pages - 1 - (j * pps + i)]
            suf = tab_ref[pid, :H_FOX, :] + carry
            carry = carry + tab_ref[pid, H_FOX:, :]
            s_parts.append(s + jnp.concatenate([suf] * t_new, axis=0) + cq)
        scores.append(jnp.concatenate(s_parts, axis=1))
    carry_ref[...] = carry
    softmaxes = []
    for s_g in scores:
        m_g = jnp.max(s_g, axis=-1, keepdims=True)
        p = jnp.exp(s_g - m_g)
        softmaxes.append((m_g, jnp.sum(p, axis=-1, keepdims=True), p.astype(BF16)))
    partials = []
    for group, (m_g, l_g, p_bf) in zip(groups, softmaxes):
        vt = jnp.concatenate([v_refs[i][...] for i in group], axis=1).astype(BF16)
        partials.append((m_g, l_g, lax.dot_general(p_bf, vt, NT_DIMS, preferred_element_type=F32)))
    m_prev = m_ref[...][:, :1]
    m_new = m_prev
    for m_g, _, _ in partials:
        m_new = jnp.maximum(m_new, m_g)
    alpha = jnp.exp(m_prev - m_new)
    l_new = alpha * l_ref[...][:, :1]
    acc = alpha * acc_ref[...]
    for m_g, l_g, acc_g in partials:
        w_g = jnp.exp(m_g - m_new)
        l_new = l_new + w_g * l_g
        acc = acc + w_g * acc_g
    m_ref[...] = jnp.broadcast_to(m_new, m_ref.shape)
    l_ref[...] = jnp.broadcast_to(l_new, l_ref.shape)
    acc_ref[...] = acc

    @pl.when(j == pl.num_programs(1) - 1)
    def _():
        o = acc_ref[...] / l_ref[...][:, :1]
        o = jnp.where(jnp.concatenate([diag] * t_new, axis=0), o, 0.0)
        y = o * lax.rsqrt(jnp.sum(o * o, axis=-1, keepdims=True) / dh + EPS)
        out = jnp.concatenate(
            [jnp.sum(y[t * H_FOX:(t + 1) * H_FOX], axis=0, keepdims=True) for t in range(t_new)],
            axis=0)
        o_ref[0] = out * gain_ref[...]


def _fox_decode(q, k_new, v_new, cq_col, gain, cache_k, cache_v, suffix_tab, page_table,
                layer, n_pool, pps=8):
    bn, t_new, d = q.shape
    page = cache_k.shape[2]
    n_pages = page_table.shape[1]
    pps = min(pps, n_pages)
    assert n_pages % pps == 0
    rows = t_new * H_FOX
    scale = float((d // H_FOX) ** -0.5)
    tok = pl.BlockSpec((1, t_new, d), lambda b, j, pt: (b, 0, 0))
    in_specs = [tok, tok, tok,
                pl.BlockSpec((1, rows, 1), lambda b, j, pt: (b, 0, 0)),
                pl.BlockSpec((1, d), lambda b, j, pt: (0, 0)),
                _resident((None, n_pool, 2 * H_FOX, page), lambda b, j, pt: (layer, 0, 0, 0)),
                pl.BlockSpec(memory_space=pl.ANY),
                pl.BlockSpec(memory_space=pl.ANY)]
    grid_spec = pltpu.PrefetchScalarGridSpec(
        num_scalar_prefetch=1,
        grid=(bn, n_pages // pps),
        in_specs=in_specs,
        out_specs=pl.BlockSpec((1, t_new, d), lambda b, j, pt: (b, 0, 0)),
        scratch_shapes=[pltpu.VMEM((rows, LANES), F32),
                        pltpu.VMEM((rows, LANES), F32),
                        pltpu.VMEM((rows, d), F32),
                        pltpu.VMEM((H_FOX, page), F32),
                        pltpu.VMEM((rows, d), F32),
                        pltpu.VMEM((DECODE_SLOTS, pps, d, page), F32),
                        pltpu.VMEM((DECODE_SLOTS, pps, d, page), F32),
                        pltpu.SemaphoreType.DMA((DECODE_SLOTS, 2, pps))],
    )
    return pl.pallas_call(
        functools.partial(_fox_decode_kernel, pps=pps, scale=scale, first_page=layer * n_pool),
        grid_spec=grid_spec,
        out_shape=jax.ShapeDtypeStruct((bn, t_new, d), F32),
        compiler_params=_params("arbitrary", "arbitrary"),
        name="fox_decode",
    )(page_table, q, k_new, v_new, cq_col, gain.reshape(1, d), suffix_tab, cache_k, cache_v)


def kernel(x_prompt, x_sample, cache_k, cache_v, cache_logf, state_hgrn, page_table, norm_ffn1, ffn1_w_in, ffn1_w_out, norm_mix, w_in_mix, hgrn_lb, fox_f_bias, hgrn_out_gain, fox_out_gain, w_out_mix, norm_ffn2, ffn2_w_in, ffn2_w_out, norm_final):
    depth = norm_ffn1.shape[0]
    bn, t_len, d = x_prompt.shape
    db, t_new, _ = x_sample.shape
    d_fox = fox_out_gain.shape[1]
    d_h = hgrn_out_gain.shape[1]
    dh_fox = d_fox // H_FOX
    n_pool, page = cache_k.shape[1], cache_k.shape[2]
    d_main = 4 * d_h + 3 * d_fox

    w1_in, w1_out = ffn1_w_in.astype(BF16), ffn1_w_out.astype(BF16)
    w2_in, w2_out = ffn2_w_in.astype(BF16), ffn2_w_out.astype(BF16)
    w_mix = w_in_mix.astype(BF16)
    w_f = jnp.pad(w_in_mix[:, :, d_main:], ((0, 0), (0, 0), (0, LANES - H_FOX))).astype(BF16)
    f_bias = jnp.pad(fox_f_bias, ((0, 0), (0, LANES - H_FOX))).reshape(depth, 1, LANES)
    w_out = w_out_mix.astype(BF16)

    ck = jnp.transpose(cache_k, (0, 1, 3, 4, 2)).reshape(depth * n_pool, d_fox, page)
    cv = jnp.transpose(cache_v, (0, 1, 3, 4, 2)).reshape(depth * n_pool, d_fox, page)
    lf_t = jnp.swapaxes(cache_logf, 2, 3).reshape(depth * n_pool, H_FOX, page)
    suffix_tab = _suffix(lf_t).reshape(depth, n_pool, 2 * H_FOX, page)

    bf16_rows = 2 * SUBLANES
    t_pad = -(-t_new // bf16_rows) * bf16_rows

    def run(x, seq_len, nseq, prompt):
        ks, vs, lfs, ss = [], [], [], []
        kv_stack = None
        for l in range(depth):
            x = _ffn(x, norm_ffn1[l], w1_in, w1_out, l)
            if prompt:
                hg, q, v, lf, cum, cumt, *kv_stack = _inproj(
                    x, norm_mix[l], w_mix, w_f, f_bias, l, seq_len, 4 * d_h, d_fox,
                    earlier=kv_stack, tm=min(512, seq_len))
                o_a, s_fin = _hgrn(hg.reshape(nseq, seq_len, 4 * d_h), hgrn_lb, hgrn_out_gain[l], None, l,
                                   tb=min(512, seq_len), chunk=64, sub=16, valid_len=None, out_dtype=BF16)
                o_a = o_a.reshape(nseq * seq_len, d_h)
                tiles = cumt.shape[0] // nseq
                cumt_b = cumt.reshape(nseq, tiles, H_FOX, -1).transpose(0, 2, 1, 3).reshape(nseq, H_FOX, seq_len)
                o_f = _fox_prompt(q.reshape(nseq, seq_len, d_fox), kv_stack[0], l,
                                  v.reshape(nseq, seq_len, d_fox), cum.reshape(nseq, seq_len, H_FOX),
                                  cumt_b, fox_out_gain[l]).reshape(nseq * seq_len, d_fox)
            else:
                hg, q, k, v, lf, cum, cumt = _inproj(x, norm_mix[l], w_mix, w_f, f_bias, l, seq_len,
                                                     4 * d_h, d_fox)
                ks.append(k.reshape(nseq, seq_len, H_FOX, dh_fox))
                vs.append(v.reshape(nseq, seq_len, H_FOX, dh_fox))
                hg_p = jnp.pad(hg.reshape(nseq, seq_len, 4 * d_h), ((0, 0), (0, t_pad - seq_len), (0, 0)))
                o_a, s_fin = _hgrn(hg_p, hgrn_lb, hgrn_out_gain[l], state_hgrn, l,
                                   tb=t_pad, chunk=t_pad, sub=t_pad, valid_len=seq_len, out_dtype=F32)
                o_a = o_a[:, :seq_len].reshape(nseq * seq_len, d_h).astype(BF16)
                o_f = _fox_decode(q.reshape(nseq, seq_len, d_fox), k.reshape(nseq, seq_len, d_fox),
                                  v.reshape(nseq, seq_len, d_fox), cum.reshape(nseq, seq_len * H_FOX, 1),
                                  fox_out_gain[l], ck, cv, suffix_tab, page_table, l, n_pool)
                o_f = o_f.reshape(nseq * seq_len, d_fox).astype(BF16)
            x = _ffn(x, norm_ffn2[l], w2_in, w2_out, l, mixer=(o_a, o_f, w_out),
                     final_g=norm_final if l == depth - 1 else None)
            lfs.append(lf.reshape(nseq, seq_len, H_FOX))
            ss.append(s_fin)
        if prompt:
            k_all, v_all = (a.reshape(depth, nseq, H_FOX, dh_fox, seq_len).transpose(0, 1, 4, 2, 3)
                            for a in kv_stack)
        else:
            k_all, v_all = jnp.stack(ks), jnp.stack(vs)
        return (x.reshape(nseq, seq_len, d), k_all, v_all, jnp.stack(lfs), jnp.stack(ss))

    y_p, k_p, v_p, lf_p, s_p = run(x_prompt.reshape(bn * t_len, d), t_len, bn, True)
    y_s, k_s, v_s, lf_s, s_s = run(x_sample.reshape(db * t_new, d), t_new, db, False)
    return (y_p, y_s, k_p, v_p, lf_p, s_p, k_s, v_s, lf_s, s_s)
```

```python
import functools

import jax
import jax.numpy as jnp
from jax import lax
from jax.experimental import pallas as pl
from jax.experimental.pallas import tpu as pltpu

F32 = jnp.float32
BF16 = jnp.bfloat16

EPS = 1e-6
TINY = 1e-30
NEG = -1e30

LOG2E = 1.4426950408889634
H_HGRN = 4
H_FOX = 8
FOX_ROW_CHUNK = 64
FOX_DIAG_BANDS = 4
CUM_BLOCK = 256
DECODE_GROUP = 2
DECODE_SLOTS = 3
LANES = 128
SUBLANES = 8
VMEM_LIMIT = 56 * 1024 * 1024

NT_DIMS = (((1,), (1,)), ((), ()))
TN_DIMS = (((0,), (0,)), ((), ()))


def _params(*sem):
    return pltpu.CompilerParams(dimension_semantics=sem, vmem_limit_bytes=VMEM_LIMIT)


def _resident(shape, index_map):
    return pl.BlockSpec(shape, index_map, pipeline_mode=pl.Buffered(1))


def _rms(x, g):
    return x * lax.rsqrt(jnp.mean(x * x, axis=-1, keepdims=True) + EPS) * g


def _split3(x):
    hi = x.astype(BF16)
    r = x - hi.astype(F32)
    mid = r.astype(BF16)
    lo = (r - mid.astype(F32)).astype(BF16)
    return hi, mid, lo


def _dot01_left(m01, x):
    return sum(jnp.dot(m01, t, preferred_element_type=F32) for t in _split3(x))


def _dot01_right(x, m01):
    return sum(jnp.dot(t, m01, preferred_element_type=F32) for t in _split3(x))


def _ffn_kernel(x_ref, g_ref, wa_ref, wb_ref, wo_ref, *rest, final_norm, mixer_out):
    rest = list(rest)
    o_ref = rest.pop()
    x = x_ref[...]
    if mixer_out:
        oa_ref, of_ref, wma_ref, wmf_ref = rest[:4]
        del rest[:4]
        x = (x + jnp.dot(oa_ref[...], wma_ref[...], preferred_element_type=F32)
             + jnp.dot(of_ref[...], wmf_ref[...], preferred_element_type=F32))
    if final_norm:
        (gf_ref,) = rest
    xn = _rms(x, g_ref[...]).astype(BF16)
    a = jnp.dot(xn, wa_ref[...], preferred_element_type=F32)
    b = jnp.dot(xn, wb_ref[...], preferred_element_type=F32)
    h = (a * jax.nn.sigmoid(a) * b).astype(BF16)
    y = x + 0.5 * jnp.dot(h, wo_ref[...], preferred_element_type=F32)
    if final_norm:
        y = _rms(y, gf_ref[...])
    o_ref[...] = y


def _ffn(x, g, w_in, w_out, layer, mixer=None, final_g=None, tm=512):
    n, d = x.shape
    ff = w_out.shape[1]
    tm = min(tm, n)
    in_specs = [
        pl.BlockSpec((tm, d), lambda i: (i, 0)),
        _resident((1, d), lambda i: (0, 0)),
        _resident((None, d, ff), lambda i: (layer, 0, 0)),
        _resident((None, d, ff), lambda i: (layer, 0, 1)),
        _resident((None, ff, d), lambda i: (layer, 0, 0)),
    ]
    args = [x, g.reshape(1, d), w_in, w_in, w_out]
    if mixer is not None:
        o_a, o_f, w_mix_out = mixer
        dm = o_a.shape[1]
        assert o_f.shape[1] == dm and w_mix_out.shape[1] == 2 * dm
        in_specs += [pl.BlockSpec((tm, dm), lambda i: (i, 0)),
                     pl.BlockSpec((tm, dm), lambda i: (i, 0)),
                     _resident((None, dm, d), lambda i: (layer, 0, 0)),
                     _resident((None, dm, d), lambda i: (layer, 1, 0))]
        args += [o_a, o_f, w_mix_out, w_mix_out]
    if final_g is not None:
        in_specs.append(_resident((1, d), lambda i: (0, 0)))
        args.append(final_g.reshape(1, d))
    return pl.pallas_call(
        functools.partial(_ffn_kernel, final_norm=final_g is not None, mixer_out=mixer is not None),
        grid=(n // tm,),
        in_specs=in_specs,
        out_specs=pl.BlockSpec((tm, d), lambda i: (i, 0)),
        out_shape=jax.ShapeDtypeStruct((n, d), F32),
        compiler_params=_params("parallel"),
        name="ffn",
    )(*args)


def _inproj_kernel(x_ref, g_ref, w_ref, wf_ref, fb_ref, *rest,
                   seq_len, d_hg, d_fox, feature_major, n_earlier):
    rest = list(rest)
    carry_ref = rest.pop()
    if n_earlier:
        pk_ref, pv_ref = rest[:2]
        del rest[:2]
    if feature_major:
        hg_ref, q_ref, v_ref, lf_ref, cum_ref, cumt_ref, kt_ref, vt_ref = rest
    else:
        hg_ref, q_ref, k_ref, v_ref, lf_ref, cum_ref, cumt_ref = rest
    tm = x_ref.shape[0]
    i = pl.program_id(0)
    xn = _rms(x_ref[...], g_ref[...]).astype(BF16)
    p = jnp.dot(xn, w_ref[...], preferred_element_type=F32)
    hg_ref[...] = p[:, :d_hg]
    q_ref[...] = p[:, d_hg:d_hg + d_fox]
    k = p[:, d_hg + d_fox:d_hg + 2 * d_fox]
    v = p[:, d_hg + 2 * d_fox:d_hg + 3 * d_fox]
    v_ref[...] = v
    if feature_major:
        if n_earlier:
            kt_ref[:n_earlier, 0] = pk_ref[:, 0]
            vt_ref[:n_earlier, 0] = pv_ref[:, 0]
        kt_ref[n_earlier, 0] = k.T
        vt_ref[n_earlier, 0] = v.T
    else:
        k_ref[...] = k

    z = jnp.dot(xn, wf_ref[...], preferred_element_type=F32) + fb_ref[...]
    lf = jnp.minimum(z, 0.0) - jnp.log1p(jnp.exp(-jnp.abs(z)))

    bs = min(CUM_BLOCK, tm)
    assert seq_len >= tm or bs % seq_len == 0
    row = lax.broadcasted_iota(jnp.int32, (bs, bs), 0)
    col = lax.broadcasted_iota(jnp.int32, (bs, bs), 1)
    same = col <= row
    if seq_len < tm:
        same = same & ((row // seq_len) == (col // seq_len))
    same = same.astype(BF16)
    offset = jnp.zeros((1, LANES), F32)
    if seq_len > tm:
        tiles_per_seq = seq_len // tm

        @pl.when(i % tiles_per_seq == 0)
        def _():
            carry_ref[...] = jnp.zeros_like(carry_ref)

        offset = carry_ref[...]
    blocks = []
    for r0 in range(0, tm, bs):
        blocks.append(_dot01_left(same, lf[r0:r0 + bs]) + offset)
        if seq_len >= tm:
            offset = blocks[-1][bs - 1:bs, :]
    cum = jnp.concatenate(blocks, axis=0) if len(blocks) > 1 else blocks[0]
    if seq_len > tm:
        carry_ref[...] = offset
    lf_ref[...] = lf[:, :H_FOX]
    cum_ref[...] = cum[:, :H_FOX]
    cumt_ref[0] = cum.T[:H_FOX, :]


def _inproj(x, g, w_all, w_f, f_bias, layer, seq_len, d_hg, d_fox, earlier=None, tm=512):
    n, d = x.shape
    tm = min(tm, n)
    assert seq_len % tm == 0 or tm % seq_len == 0
    d_main = d_hg + 3 * d_fox
    nt = n // tm
    tok = lambda w: pl.BlockSpec((tm, w), lambda i: (i, 0))
    tok_shape = lambda w: jax.ShapeDtypeStruct((n, w), F32)
    feature_major = seq_len % tm == 0
    n_earlier = 0 if earlier is None else earlier[0].shape[0]
    in_specs = [
        tok(d),
        _resident((1, d), lambda i: (0, 0)),
        _resident((None, d, d_main), lambda i: (layer, 0, 0)),
        _resident((None, d, LANES), lambda i: (layer, 0, 0)),
        _resident((None, 1, LANES), lambda i: (layer, 0, 0)),
    ]
    args = [x, g.reshape(1, d), w_all, w_f, f_bias]
    tail_specs = [tok(H_FOX), tok(H_FOX), pl.BlockSpec((1, H_FOX, tm), lambda i: (i, 0, 0))]
    tail_shape = [tok_shape(H_FOX), tok_shape(H_FOX), jax.ShapeDtypeStruct((nt, H_FOX, tm), F32)]
    if feature_major:
        tps = seq_len // tm
        stack = lambda layers: pl.BlockSpec((layers, 1, d_fox, tm), lambda i: (0, i // tps, 0, i % tps))
        if n_earlier:
            in_specs += [stack(n_earlier)] * 2
            args += list(earlier)
        out_specs = [tok(d_hg), tok(d_fox), tok(d_fox)] + tail_specs + [stack(n_earlier + 1)] * 2
        out_shape = ([tok_shape(d_hg), tok_shape(d_fox), tok_shape(d_fox)] + tail_shape
                     + [jax.ShapeDtypeStruct((n_earlier + 1, n // seq_len, d_fox, seq_len), F32)] * 2)
    else:
        assert earlier is None
        out_specs = [tok(d_hg), tok(d_fox), tok(d_fox), tok(d_fox)] + tail_specs
        out_shape = [tok_shape(d_hg), tok_shape(d_fox), tok_shape(d_fox), tok_shape(d_fox)] + tail_shape
    return pl.pallas_call(
        functools.partial(_inproj_kernel, seq_len=seq_len, d_hg=d_hg, d_fox=d_fox,
                          feature_major=feature_major, n_earlier=n_earlier),
        grid=(nt,),
        in_specs=in_specs,
        out_specs=out_specs,
        out_shape=out_shape,
        scratch_shapes=[pltpu.VMEM((1, LANES), F32)],
        compiler_params=_params("arbitrary"),
        name="inproj",
    )(*args)


def _hgrn_kernel(lbp_ref, aq_ref, af_ref, ai_ref, ag_ref, gain_ref, *rest,
                 layer, chunk, sub, valid_len, zero_init):
    if zero_init:
        o_ref, sout_ref, st_ref, ks_ref, as_ref = rest
    else:
        s0_ref, o_ref, sout_ref, st_ref, ks_ref, as_ref = rest
    tb = aq_ref.shape[1]
    nh = aq_ref.shape[2] // LANES
    nchunks = tb // chunk
    t = pl.program_id(1)

    @pl.when(t == 0)
    def _():
        for h in range(nh):
            if zero_init:
                st_ref[h] = jnp.zeros((LANES, LANES), F32)
            else:
                st_ref[h] = s0_ref[0, 0, h].T

    lbp = lbp_ref[...]
    e = jnp.exp(lbp - jnp.max(lbp, axis=0, keepdims=True))
    prob = e / jnp.sum(e, axis=0, keepdims=True)
    lb = jnp.zeros((1, nh * LANES), F32)
    for j in range(1, layer + 1):
        lb = lb + prob[j:j + 1, :]

    nsub = chunk // sub
    row = lax.broadcasted_iota(jnp.int32, (chunk, 1), 0)
    rmod = row % sub
    lane_off = lax.broadcasted_iota(jnp.int32, (chunk, LANES), 1) - (row - rmod)
    tri = (lax.broadcasted_iota(jnp.int32, (chunk, chunk), 1)
           <= lax.broadcasted_iota(jnp.int32, (chunk, chunk), 0)).astype(BF16)
    zpad_bf = jnp.zeros((LANES - chunk, LANES), BF16)
    gain_all = gain_ref[...]

    def chunk_body(c, carry):
        r0 = pl.multiple_of(c * chunk, chunk)
        rows = pl.ds(r0, chunk)
        f_all = lb + (1.0 - lb) * jax.nn.sigmoid(af_ref[0, rows, :])
        logf_all = jnp.log(jnp.maximum(f_all, TINY))
        k_all = 1.0 - f_all
        if valid_len is not None:
            live = (t * tb + r0 + row) < valid_len
            logf_all = jnp.where(live, logf_all, 0.0)
            k_all = jnp.where(live, k_all, 0.0)
        a_all = _dot01_left(tri, logf_all) * LOG2E
        ks_ref[...] = k_all
        as_ref[...] = a_all

        for h in range(nh):
            hs = slice(h * LANES, (h + 1) * LANES)
            q = aq_ref[0, rows, hs]
            v = ai_ref[0, rows, hs]
            g = ag_ref[0, rows, hs]
            k, a = k_all[:, hs], a_all[:, hs]
            st = st_ref[h]
            v_bf = v.astype(BF16)

            o = lax.dot_general((q * jnp.exp2(a)).astype(BF16), st.astype(BF16), NT_DIMS,
                                preferred_element_type=F32)

            blocks = [jnp.zeros((sub, LANES), F32)]
            for i in range(1, nsub):
                lo, hi = i * sub, (i + 1) * sub
                r = a[lo - 1:lo, :]
                qt = (q[lo:hi] * jnp.exp2(a[lo:hi] - r)).astype(BF16)
                kt = jnp.where(row < lo, k * jnp.exp2(r - a), 0.0).astype(BF16)
                blocks.append(lax.dot_general(qt, jnp.concatenate([kt, zpad_bf], axis=0), NT_DIMS,
                                              preferred_element_type=F32))
            sc = jnp.concatenate(blocks, axis=0) if nsub > 1 else blocks[0]
            for j in range(sub):
                kj = jnp.concatenate([jnp.broadcast_to(ks_ref[i * sub + j:i * sub + j + 1, hs], (sub, LANES))
                                      for i in range(nsub)], axis=0)
                aj = jnp.concatenate([jnp.broadcast_to(as_ref[i * sub + j:i * sub + j + 1, hs], (sub, LANES))
                                      for i in range(nsub)], axis=0)
                w = q * kj * jnp.exp2(a - aj)
                col = jnp.where(rmod >= j, jnp.sum(w, axis=-1, keepdims=True), 0.0)
                sc = jnp.where(lane_off == j, col, sc)
            o = o + jnp.dot(sc.astype(BF16), jnp.concatenate([v_bf, zpad_bf], axis=0),
                            preferred_element_type=F32)

            a_last = a[chunk - 1:chunk, :]
            kt = (k * jnp.exp2(a_last - a)).astype(BF16)
            st_ref[h] = st * jnp.exp2(a_last) + lax.dot_general(
                v_bf, kt, TN_DIMS, preferred_element_type=F32)

            y = o * lax.rsqrt(jnp.mean(o * o, axis=-1, keepdims=True) + EPS) * gain_all[:, hs]
            y = y * (g * jax.nn.sigmoid(g))
            o_ref[0, rows, hs] = y.astype(o_ref.dtype)
        return carry

    lax.fori_loop(0, nchunks, chunk_body, 0)

    @pl.when(t == pl.num_programs(1) - 1)
    def _():
        for h in range(nh):
            sout_ref[0, h] = st_ref[h].T


def _hgrn(hg, lb_param, gain, state0, layer, *, tb, chunk, sub, valid_len, out_dtype):
    bn, t_len, d4 = hg.shape
    dh = d4 // 4
    nh = dh // LANES
    depth = lb_param.shape[0]
    blk = lambda c: pl.BlockSpec((1, tb, dh), lambda b, t: (b, t, c))
    in_specs = [pl.BlockSpec((depth, dh), lambda b, t: (0, 0)),
                blk(0), blk(1), blk(2), blk(3),
                pl.BlockSpec((1, dh), lambda b, t: (0, 0))]
    args = [lb_param, hg, hg, hg, hg, gain.reshape(1, dh)]
    if state0 is not None:
        in_specs.append(pl.BlockSpec((1, 1, nh, LANES, LANES), lambda b, t: (layer, b, 0, 0, 0)))
        args.append(state0)
    return pl.pallas_call(
        functools.partial(_hgrn_kernel, layer=layer, chunk=chunk, sub=sub,
                          valid_len=valid_len, zero_init=state0 is None),
        grid=(bn, t_len // tb),
        in_specs=in_specs,
        out_specs=[pl.BlockSpec((1, tb, dh), lambda b, t: (b, t, 0)),
                   pl.BlockSpec((1, nh, LANES, LANES), lambda b, t: (b, 0, 0, 0))],
        out_shape=[jax.ShapeDtypeStruct((bn, t_len, dh), out_dtype),
                   jax.ShapeDtypeStruct((bn, nh, LANES, LANES), F32)],
        scratch_shapes=[pltpu.VMEM((nh, LANES, LANES), F32),
                        pltpu.VMEM((chunk, dh), F32),
                        pltpu.VMEM((chunk, dh), F32)],
        compiler_params=_params("parallel", "arbitrary"),
        name="hgrn",
    )(*args)


def _fox_prompt_kernel(q_ref, kt_ref, v_ref, cum_ref, cumt_ref, gain_ref, o_ref,
                       s_ref, p_ref, fill_ref, m_ref, l_ref, acc_ref, *, scale):
    tq = q_ref.shape[1]
    rc = FOX_ROW_CHUNK
    band = max(tq // FOX_DIAG_BANDS, LANES)
    assert tq % band == 0 and band % rc == 0 and band % LANES == 0
    dh = LANES // 2
    hp = pl.program_id(1)
    qi = pl.program_id(2)
    lane = lax.broadcasted_iota(jnp.int32, (1, LANES), 1)
    q = q_ref[0] * (scale * LOG2E)
    cum = cum_ref[0]
    hl = lax.broadcasted_iota(jnp.int32, (1, H_FOX), 1)

    heads = range(2)
    qms = []
    for hh in heads:
        qms.append(jnp.where((lane // dh) == hh, q, 0.0).astype(BF16))
        cq = jnp.sum(jnp.where(hl == 2 * hp + hh, cum, 0.0), axis=-1, keepdims=True)
        fill_ref[hh] = jnp.broadcast_to((NEG - cq) * LOG2E, (tq, LANES))
        m_ref[hh] = jnp.full((tq, LANES), -jnp.inf, F32)
        l_ref[hh] = jnp.zeros((tq, LANES), F32)
        acc_ref[hh] = jnp.zeros((tq, LANES), F32)

    def key_bias(hh, kb):
        k0 = pl.multiple_of(kb * tq, tq)
        return cumt_ref[0, pl.ds(2 * hp + hh, 1), pl.ds(k0, tq)] * LOG2E

    def scores(kb, masked):
        k0 = pl.multiple_of(kb * tq, tq)
        kt = kt_ref[0, :, pl.ds(k0, tq)].astype(BF16)
        for hh in heads:
            if masked:
                for r1 in range(band, tq + 1, band):
                    s_ref[hh, kb, r1 - band:r1, :r1] = jnp.dot(
                        qms[hh][r1 - band:r1], kt[:, :r1], preferred_element_type=F32)
            else:
                s_ref[hh, kb] = jnp.dot(qms[hh], kt, preferred_element_type=F32)
        for hh in heads:
            ck = key_bias(hh, kb)
            for r in range(tq // rc):
                rs = slice(r * rc, (r + 1) * rc)
                m = m_ref[hh, rs, :]
                for j in range(tq // LANES):
                    cs = slice(j * LANES, (j + 1) * LANES)
                    if masked and j * LANES > (r + 1) * rc - 1:
                        continue
                    s = s_ref[hh, kb, rs, cs] - ck[:, cs]
                    if masked and (j + 1) * LANES - 1 > r * rc:
                        rid = r * rc + lax.broadcasted_iota(jnp.int32, (rc, LANES), 0)
                        cid = j * LANES + lax.broadcasted_iota(jnp.int32, (rc, LANES), 1)
                        s = jnp.where(rid >= cid, s, fill_ref[hh, rs, :])
                    if masked:
                        s_ref[hh, kb, rs, cs] = s
                    m = jnp.maximum(m, s)
                m_ref[hh, rs, :] = m

    def weights(kb, masked):
        k0 = pl.multiple_of(kb * tq, tq)
        v = v_ref[0, pl.ds(k0, tq), :].astype(BF16)
        for hh in heads:
            ck = key_bias(hh, kb)
            for r in range(tq // rc):
                rs = slice(r * rc, (r + 1) * rc)
                m = m_ref[hh, rs, :]
                lsum = l_ref[hh, rs, :]
                for j in range(tq // LANES):
                    cs = slice(j * LANES, (j + 1) * LANES)
                    if masked and j * LANES > (r + 1) * rc - 1:
                        if j * LANES < -(-(r + 1) * rc // band) * band:
                            p_ref[hh, rs, cs] = jnp.zeros((rc, LANES), BF16)
                        continue
                    s = s_ref[hh, kb, rs, cs]
                    if not masked:
                        s = s - ck[:, cs]
                    p = jnp.exp2(s - m)
                    lsum = lsum + p
                    p_ref[hh, rs, cs] = p.astype(BF16)
                l_ref[hh, rs, :] = lsum
            if masked:
                for r1 in range(band, tq + 1, band):
                    acc_ref[hh, r1 - band:r1, :] += jnp.dot(p_ref[hh, r1 - band:r1, :r1], v[:r1],
                                                            preferred_element_type=F32)
            else:
                acc_ref[hh] += jnp.dot(p_ref[hh], v, preferred_element_type=F32)

    def loop(fn):
        def body(kb, carry):
            fn(kb, masked=False)
            return carry
        lax.fori_loop(0, qi, body, 0)
        fn(qi, masked=True)

    loop(scores)
    for hh in heads:
        m_ref[hh] = jnp.broadcast_to(jnp.max(m_ref[hh], axis=-1, keepdims=True), (tq, LANES))
    loop(weights)

    low = (lane // dh) == 0
    o = jnp.where(low, acc_ref[0] / jnp.sum(l_ref[0], axis=-1, keepdims=True),
                  acc_ref[1] / jnp.sum(l_ref[1], axis=-1, keepdims=True))
    sq = o * o
    ss0 = jnp.sum(jnp.where(low, sq, 0.0), axis=-1, keepdims=True)
    ss1 = jnp.sum(jnp.where(low, 0.0, sq), axis=-1, keepdims=True)
    inv = jnp.where(low, lax.rsqrt(ss0 / dh + EPS), lax.rsqrt(ss1 / dh + EPS))
    o_ref[0] = (o * inv * gain_ref[...]).astype(o_ref.dtype)


def _fox_prompt(q, kt_stack, layer, v, cum, cumt, gain, tq=1024):
    bn, t_len, d = q.shape
    tq = min(tq, t_len)
    nhp = d // LANES
    scale = float((LANES // 2) ** -0.5)
    return pl.pallas_call(
        functools.partial(_fox_prompt_kernel, scale=scale),
        grid=(bn, nhp, t_len // tq),
        in_specs=[
            pl.BlockSpec((1, tq, LANES), lambda b, h, i: (b, i, h)),
            pl.BlockSpec((None, 1, LANES, t_len), lambda b, h, i: (layer, b, h, 0)),
            pl.BlockSpec((1, t_len, LANES), lambda b, h, i: (b, 0, h)),
            pl.BlockSpec((1, tq, H_FOX), lambda b, h, i: (b, i, 0)),
            pl.BlockSpec((1, H_FOX, t_len), lambda b, h, i: (b, 0, 0)),
            pl.BlockSpec((1, LANES), lambda b, h, i: (0, h)),
        ],
        out_specs=pl.BlockSpec((1, tq, LANES), lambda b, h, i: (b, i, h)),
        out_shape=jax.ShapeDtypeStruct((bn, t_len, d), BF16),
        scratch_shapes=[pltpu.VMEM((2, t_len // tq, tq, tq), F32),
                        pltpu.VMEM((2, tq, tq), BF16),
                        pltpu.VMEM((2, tq, LANES), F32),
                        pltpu.VMEM((2, tq, LANES), F32),
                        pltpu.VMEM((2, tq, LANES), F32),
                        pltpu.VMEM((2, tq, LANES), F32)],
        compiler_params=_params("parallel", "parallel", "arbitrary"),
        name="fox_prompt",
    )(q, kt_stack, v, cum, cumt, gain.reshape(1, d))


def _suffix_kernel(lf_ref, tab_ref):
    tp, nh, n = lf_ref.shape
    i = lax.broadcasted_iota(jnp.int32, (n, n), 0)
    j = lax.broadcasted_iota(jnp.int32, (n, n), 1)
    lf = lf_ref[...].reshape(tp * nh, n)
    suf = _dot01_right(lf, (i > j).astype(BF16))
    tot = _dot01_right(lf, jnp.ones((n, n), BF16))
    tab_ref[:, :nh, :] = suf.reshape(tp, nh, n)
    tab_ref[:, nh:, :] = tot.reshape(tp, nh, n)


def _suffix(lf_t, tp=512):
    pages, nh, n = lf_t.shape
    tp = min(tp, pages)
    assert pages % tp == 0
    return pl.pallas_call(
        _suffix_kernel,
        grid=(pages // tp,),
        in_specs=[pl.BlockSpec((tp, nh, n), lambda i: (i, 0, 0))],
        out_specs=pl.BlockSpec((tp, 2 * nh, n), lambda i: (i, 0, 0)),
        out_shape=jax.ShapeDtypeStruct((pages, 2 * nh, n), F32),
        compiler_params=_params("parallel"),
        name="suffix",
    )(lf_t)


def _fox_decode_kernel(pt_ref, q_ref, kn_ref, vn_ref, cq_ref, gain_ref, tab_ref, ck_hbm, cv_hbm,
                       o_ref, m_ref, l_ref, acc_ref, carry_ref, qbd_ref, kbuf, vbuf, sem,
                       *, pps, scale, first_page):
    steps = pl.num_programs(1)
    n_pages = steps * pps
    b = pl.program_id(0)
    j = pl.program_id(1)
    t_new = q_ref.shape[1]
    d = q_ref.shape[2]
    dh = d // H_FOX
    rows = t_new * H_FOX
    page = kbuf.shape[3]

    g = b * steps + j
    total = pl.num_programs(0) * steps

    def page_copies(gg, slot):
        bb = gg // steps
        jj = gg - bb * steps
        copies = []
        for i in range(pps):
            pid = first_page + pt_ref[bb, n_pages - 1 - (jj * pps + i)]
            copies.append(pltpu.make_async_copy(ck_hbm.at[pid], kbuf.at[slot, i], sem.at[slot, 0, i]))
            copies.append(pltpu.make_async_copy(cv_hbm.at[pid], vbuf.at[slot, i], sem.at[slot, 1, i]))
        return copies

    @pl.when(g == 0)
    def _():
        for ahead in range(DECODE_SLOTS - 1):
            @pl.when(ahead < total)
            def _():
                for c in page_copies(ahead, ahead):
                    c.start()

    @pl.when(g + DECODE_SLOTS - 1 < total)
    def _():
        nxt = g + DECODE_SLOTS - 1
        for c in page_copies(nxt, nxt % DECODE_SLOTS):
            c.start()

    slot = g % DECODE_SLOTS
    for c in page_copies(g, slot):
        c.wait()
    k_refs = [kbuf.at[slot, i] for i in range(pps)]
    v_refs = [vbuf.at[slot, i] for i in range(pps)]
    head_of_lane = lax.broadcasted_iota(jnp.int32, (H_FOX, d), 1) // dh
    head_of_row = lax.broadcasted_iota(jnp.int32, (H_FOX, d), 0)
    diag = head_of_lane == head_of_row
    cq = cq_ref[0]

    @pl.when(j == 0)
    def _():
        qbd = jnp.concatenate(
            [jnp.where(diag, q_ref[0, t:t + 1, :] * scale, 0.0) for t in range(t_new)], axis=0)
        qbd_ref[...] = qbd
        qrow = lax.broadcasted_iota(jnp.int32, (rows, 1), 0) // H_FOX
        s_new = []
        for s in range(t_new):
            dot_s = jnp.sum(qbd * kn_ref[0, s:s + 1, :], axis=-1, keepdims=True)
            cs = jnp.concatenate([cq[s * H_FOX:(s + 1) * H_FOX]] * t_new, axis=0)
            s_new.append(jnp.where(qrow >= s, dot_s + cq - cs, NEG))
        m0 = s_new[0]
        for s in range(1, t_new):
            m0 = jnp.maximum(m0, s_new[s])
        l0 = jnp.zeros((rows, 1), F32)
        acc0 = jnp.zeros((rows, d), F32)
        for s in range(t_new):
            p_s = jnp.exp(s_new[s] - m0)
            l0 = l0 + p_s
            acc0 = acc0 + p_s * vn_ref[0, s:s + 1, :]
        m_ref[...] = jnp.broadcast_to(m0, (rows, LANES))
        l_ref[...] = jnp.broadcast_to(l0, (rows, LANES))
        acc_ref[...] = acc0
        carry_ref[...] = jnp.zeros_like(carry_ref)

    qbd = qbd_ref[...].astype(BF16)
    carry = carry_ref[...]
    groups = [range(g0, min(g0 + DECODE_GROUP, pps)) for g0 in range(0, pps, DECODE_GROUP)]
    scores = []
    for group in groups:
        s_parts = []
        for i in group:
            kpg = k_refs[i][...].astype(BF16)
            s = jnp.dot(qbd, kpg, preferred_element_type=F32)
            pid = pt_ref[b, n_pages - 1 - (j * pps + i)]
            suf = tab_ref[pid, :H_FOX, :] + carry
            carry = carry + tab_ref[pid, H_FOX:, :]
            s_parts.append(s + jnp.concatenate([suf] * t_new, axis=0) + cq)
        scores.append(jnp.concatenate(s_parts, axis=1))
    carry_ref[...] = carry
    softmaxes = []
    for s_g in scores:
        m_g = jnp.max(s_g, axis=-1, keepdims=True)
        p = jnp.exp(s_g - m_g)
        softmaxes.append((m_g, jnp.sum(p, axis=-1, keepdims=True), p.astype(BF16)))
    partials = []
    for group, (m_g, l_g, p_bf) in zip(groups, softmaxes):
        vt = jnp.concatenate([v_refs[i][...] for i in group], axis=1).astype(BF16)
        partials.append((m_g, l_g, lax.dot_general(p_bf, vt, NT_DIMS, preferred_element_type=F32)))
    m_prev = m_ref[...][:, :1]
    m_new = m_prev
    for m_g, _, _ in partials:
        m_new = jnp.maximum(m_new, m_g)
    alpha = jnp.exp(m_prev - m_new)
    l_new = alpha * l_ref[...][:, :1]
    acc = alpha * acc_ref[...]
    for m_g, l_g, acc_g in partials:
        w_g = jnp.exp(m_g - m_new)
        l_new = l_new + w_g * l_g
        acc = acc + w_g * acc_g
    m_ref[...] = jnp.broadcast_to(m_new, m_ref.shape)
    l_ref[...] = jnp.broadcast_to(l_new, l_ref.shape)
    acc_ref[...] = acc

    @pl.when(j == pl.num_programs(1) - 1)
    def _():
        o = acc_ref[...] / l_ref[...][:, :1]
        o = jnp.where(jnp.concatenate([diag] * t_new, axis=0), o, 0.0)
        y = o * lax.rsqrt(jnp.sum(o * o, axis=-1, keepdims=True) / dh + EPS)
        out = jnp.concatenate(
            [jnp.sum(y[t * H_FOX:(t + 1) * H_FOX], axis=0, keepdims=True) for t in range(t_new)],
            axis=0)
        o_ref[0] = out * gain_ref[...]


def _fox_decode(q, k_new, v_new, cq_col, gain, cache_k, cache_v, suffix_tab, page_table,
                layer, n_pool, pps=8):
    bn, t_new, d = q.shape
    page = cache_k.shape[2]
    n_pages = page_table.shape[1]
    pps = min(pps, n_pages)
    assert n_pages % pps == 0
    rows = t_new * H_FOX
    scale = float((d // H_FOX) ** -0.5)
    tok = pl.BlockSpec((1, t_new, d), lambda b, j, pt: (b, 0, 0))
    in_specs = [tok, tok, tok,
                pl.BlockSpec((1, rows, 1), lambda b, j, pt: (b, 0, 0)),
                pl.BlockSpec((1, d), lambda b, j, pt: (0, 0)),
                _resident((None, n_pool, 2 * H_FOX, page), lambda b, j, pt: (layer, 0, 0, 0)),
                pl.BlockSpec(memory_space=pl.ANY),
                pl.BlockSpec(memory_space=pl.ANY)]
    grid_spec = pltpu.PrefetchScalarGridSpec(
        num_scalar_prefetch=1,
        grid=(bn, n_pages // pps),
        in_specs=in_specs,
        out_specs=pl.BlockSpec((1, t_new, d), lambda b, j, pt: (b, 0, 0)),
        scratch_shapes=[pltpu.VMEM((rows, LANES), F32),
                        pltpu.VMEM((rows, LANES), F32),
                        pltpu.VMEM((rows, d), F32),
                        pltpu.VMEM((H_FOX, page), F32),
                        pltpu.VMEM((rows, d), F32),
                        pltpu.VMEM((DECODE_SLOTS, pps, d, page), F32),
                        pltpu.VMEM((DECODE_SLOTS, pps, d, page), F32),
                        pltpu.SemaphoreType.DMA((DECODE_SLOTS, 2, pps))],
    )
    return pl.pallas_call(
        functools.partial(_fox_decode_kernel, pps=pps, scale=scale, first_page=layer * n_pool),
        grid_spec=grid_spec,
        out_shape=jax.ShapeDtypeStruct((bn, t_new, d), F32),
        compiler_params=_params("arbitrary", "arbitrary"),
        name="fox_decode",
    )(page_table, q, k_new, v_new, cq_col, gain.reshape(1, d), suffix_tab, cache_k, cache_v)


def kernel(x_prompt, x_sample, cache_k, cache_v, cache_logf, state_hgrn, page_table, norm_ffn1, ffn1_w_in, ffn1_w_out, norm_mix, w_in_mix, hgrn_lb, fox_f_bias, hgrn_out_gain, fox_out_gain, w_out_mix, norm_ffn2, ffn2_w_in, ffn2_w_out, norm_final):
    depth = norm_ffn1.shape[0]
    bn, t_len, d = x_prompt.shape
    db, t_new, _ = x_sample.shape
    d_fox = fox_out_gain.shape[1]
    d_h = hgrn_out_gain.shape[1]
    dh_fox = d_fox // H_FOX
    n_pool, page = cache_k.shape[1], cache_k.shape[2]
    d_main = 4 * d_h + 3 * d_fox

    w1_in, w1_out = ffn1_w_in.astype(BF16), ffn1_w_out.astype(BF16)
    w2_in, w2_out = ffn2_w_in.astype(BF16), ffn2_w_out.astype(BF16)
    w_mix = w_in_mix.astype(BF16)
    w_f = jnp.pad(w_in_mix[:, :, d_main:], ((0, 0), (0, 0), (0, LANES - H_FOX))).astype(BF16)
    f_bias = jnp.pad(fox_f_bias, ((0, 0), (0, LANES - H_FOX))).reshape(depth, 1, LANES)
    w_out = w_out_mix.astype(BF16)

    ck = jnp.transpose(cache_k, (0, 1, 3, 4, 2)).reshape(depth * n_pool, d_fox, page)
    cv = jnp.transpose(cache_v, (0, 1, 3, 4, 2)).reshape(depth * n_pool, d_fox, page)
    lf_t = jnp.swapaxes(cache_logf, 2, 3).reshape(depth * n_pool, H_FOX, page)
    suffix_tab = _suffix(lf_t).reshape(depth, n_pool, 2 * H_FOX, page)

    bf16_rows = 2 * SUBLANES
    t_pad = -(-t_new // bf16_rows) * bf16_rows

    def run(x, seq_len, nseq, prompt):
        ks, vs, lfs, ss = [], [], [], []
        kv_stack = None
        for l in range(depth):
            x = _ffn(x, norm_ffn1[l], w1_in, w1_out, l)
            if prompt:
                hg, q, v, lf, cum, cumt, *kv_stack = _inproj(
                    x, norm_mix[l], w_mix, w_f, f_bias, l, seq_len, 4 * d_h, d_fox,
                    earlier=kv_stack, tm=min(512, seq_len))
                o_a, s_fin = _hgrn(hg.reshape(nseq, seq_len, 4 * d_h), hgrn_lb, hgrn_out_gain[l], None, l,
                                   tb=min(512, seq_len), chunk=64, sub=16, valid_len=None, out_dtype=BF16)
                o_a = o_a.reshape(nseq * seq_len, d_h)
                tiles = cumt.shape[0] // nseq
                cumt_b = cumt.reshape(nseq, tiles, H_FOX, -1).transpose(0, 2, 1, 3).reshape(nseq, H_FOX, seq_len)
                o_f = _fox_prompt(q.reshape(nseq, seq_len, d_fox), kv_stack[0], l,
                                  v.reshape(nseq, seq_len, d_fox), cum.reshape(nseq, seq_len, H_FOX),
                                  cumt_b, fox_out_gain[l]).reshape(nseq * seq_len, d_fox)
            else:
                hg, q, k, v, lf, cum, cumt = _inproj(x, norm_mix[l], w_mix, w_f, f_bias, l, seq_len,
                                                     4 * d_h, d_fox)
                ks.append(k.reshape(nseq, seq_len, H_FOX, dh_fox))
                vs.append(v.reshape(nseq, seq_len, H_FOX, dh_fox))
                hg_p = jnp.pad(hg.reshape(nseq, seq_len, 4 * d_h), ((0, 0), (0, t_pad - seq_len), (0, 0)))
                o_a, s_fin = _hgrn(hg_p, hgrn_lb, hgrn_out_gain[l], state_hgrn, l,
                                   tb=t_pad, chunk=t_pad, sub=t_pad, valid_len=seq_len, out_dtype=F32)
                o_a = o_a[:, :seq_len].reshape(nseq * seq_len, d_h).astype(BF16)
                o_f = _fox_decode(q.reshape(nseq, seq_len, d_fox), k.reshape(nseq, seq_len, d_fox),
                                  v.reshape(nseq, seq_len, d_fox), cum.reshape(nseq, seq_len * H_FOX, 1),
                                  fox_out_gain[l], ck, cv, suffix_tab, page_table, l, n_pool)
                o_f = o_f.reshape(nseq * seq_len, d_fox).astype(BF16)
            x = _ffn(x, norm_ffn2[l], w2_in, w2_out, l, mixer=(o_a, o_f, w_out),
                     final_g=norm_final if l == depth - 1 else None)
            lfs.append(lf.reshape(nseq, seq_len, H_FOX))
            ss.append(s_fin)
        if prompt:
            k_all, v_all = (a.reshape(depth, nseq, H_FOX, dh_fox, seq_len).transpose(0, 1, 4, 2, 3)
                            for a in kv_stack)
        else:
            k_all, v_all = jnp.stack(ks), jnp.stack(vs)
        return (x.reshape(nseq, seq_len, d), k_all, v_all, jnp.stack(lfs), jnp.stack(ss))

    y_p, k_p, v_p, lf_p, s_p = run(x_prompt.reshape(bn * t_len, d), t_len, bn, True)
    y_s, k_s, v_s, lf_s, s_s = run(x_sample.reshape(db * t_new, d), t_new, db, False)
    return (y_p, y_s, k_p, v_p, lf_p, s_p, k_s, v_s, lf_s, s_s)
```

```python
import functools

import jax
import jax.numpy as jnp
from jax import lax
from jax.experimental import pallas as pl
from jax.experimental.pallas import tpu as pltpu

F32 = jnp.float32
BF16 = jnp.bfloat16

EPS = 1e-6
TINY = 1e-30
NEG = -1e30

LOG2E = 1.4426950408889634
H_HGRN = 4
H_FOX = 8
FOX_ROW_CHUNK = 64
FOX_DIAG_BANDS = 4
CUM_BLOCK = 256
HGRN_SPAN_LIMIT = 100.0
DECODE_GROUP = 2
DECODE_SLOTS = 3
LANES = 128
SUBLANES = 8
VMEM_LIMIT = 56 * 1024 * 1024

NT_DIMS = (((1,), (1,)), ((), ()))
TN_DIMS = (((0,), (0,)), ((), ()))


def _params(*sem):
    return pltpu.CompilerParams(dimension_semantics=sem, vmem_limit_bytes=VMEM_LIMIT)


def _resident(shape, index_map):
    return pl.BlockSpec(shape, index_map, pipeline_mode=pl.Buffered(1))


def _rms(x, g):
    return x * lax.rsqrt(jnp.mean(x * x, axis=-1, keepdims=True) + EPS) * g


def _split3(x):
    hi = x.astype(BF16)
    r = x - hi.astype(F32)
    mid = r.astype(BF16)
    lo = (r - mid.astype(F32)).astype(BF16)
    return hi, mid, lo


def _dot01_left(m01, x):
    return sum(jnp.dot(m01, t, preferred_element_type=F32) for t in _split3(x))


def _dot01_right(x, m01):
    return sum(jnp.dot(t, m01, preferred_element_type=F32) for t in _split3(x))


def _ffn_kernel(x_ref, g_ref, wa_ref, wb_ref, wo_ref, *rest, final_norm, mixer_out):
    rest = list(rest)
    o_ref = rest.pop()
    x = x_ref[...]
    if mixer_out:
        oa_ref, of_ref, wma_ref, wmf_ref = rest[:4]
        del rest[:4]
        x = (x + jnp.dot(oa_ref[...], wma_ref[...], preferred_element_type=F32)
             + jnp.dot(of_ref[...], wmf_ref[...], preferred_element_type=F32))
    if final_norm:
        (gf_ref,) = rest
    xn = _rms(x, g_ref[...]).astype(BF16)
    a = jnp.dot(xn, wa_ref[...], preferred_element_type=F32)
    b = jnp.dot(xn, wb_ref[...], preferred_element_type=F32)
    h = (a * jax.nn.sigmoid(a) * b).astype(BF16)
    y = x + 0.5 * jnp.dot(h, wo_ref[...], preferred_element_type=F32)
    if final_norm:
        y = _rms(y, gf_ref[...])
    o_ref[...] = y


def _ffn(x, g, w_in, w_out, layer, mixer=None, final_g=None, tm=512):
    n, d = x.shape
    ff = w_out.shape[1]
    tm = min(tm, n)
    in_specs = [
        pl.BlockSpec((tm, d), lambda i: (i, 0)),
        _resident((1, d), lambda i: (0, 0)),
        _resident((None, d, ff), lambda i: (layer, 0, 0)),
        _resident((None, d, ff), lambda i: (layer, 0, 1)),
        _resident((None, ff, d), lambda i: (layer, 0, 0)),
    ]
    args = [x, g.reshape(1, d), w_in, w_in, w_out]
    if mixer is not None:
        o_a, o_f, w_mix_out = mixer
        dm = o_a.shape[1]
        assert o_f.shape[1] == dm and w_mix_out.shape[1] == 2 * dm
        in_specs += [pl.BlockSpec((tm, dm), lambda i: (i, 0)),
                     pl.BlockSpec((tm, dm), lambda i: (i, 0)),
                     _resident((None, dm, d), lambda i: (layer, 0, 0)),
                     _resident((None, dm, d), lambda i: (layer, 1, 0))]
        args += [o_a, o_f, w_mix_out, w_mix_out]
    if final_g is not None:
        in_specs.append(_resident((1, d), lambda i: (0, 0)))
        args.append(final_g.reshape(1, d))
    return pl.pallas_call(
        functools.partial(_ffn_kernel, final_norm=final_g is not None, mixer_out=mixer is not None),
        grid=(n // tm,),
        in_specs=in_specs,
        out_specs=pl.BlockSpec((tm, d), lambda i: (i, 0)),
        out_shape=jax.ShapeDtypeStruct((n, d), F32),
        compiler_params=_params("parallel"),
        name="ffn",
    )(*args)


def _inproj_kernel(x_ref, g_ref, w_ref, wf_ref, fb_ref, *rest,
                   seq_len, d_hg, d_fox, feature_major, n_earlier):
    rest = list(rest)
    carry_ref = rest.pop()
    if n_earlier:
        pk_ref, pv_ref = rest[:2]
        del rest[:2]
    if feature_major:
        hg_ref, q_ref, v_ref, lf_ref, cum_ref, cumt_ref, kt_ref, vt_ref = rest
    else:
        hg_ref, q_ref, k_ref, v_ref, lf_ref, cum_ref, cumt_ref = rest
    tm = x_ref.shape[0]
    i = pl.program_id(0)
    xn = _rms(x_ref[...], g_ref[...]).astype(BF16)
    p = jnp.dot(xn, w_ref[...], preferred_element_type=F32)
    hg_ref[...] = p[:, :d_hg]
    q_ref[...] = p[:, d_hg:d_hg + d_fox]
    k = p[:, d_hg + d_fox:d_hg + 2 * d_fox]
    v = p[:, d_hg + 2 * d_fox:d_hg + 3 * d_fox]
    v_ref[...] = v
    if feature_major:
        if n_earlier:
            kt_ref[:n_earlier, 0] = pk_ref[:, 0]
            vt_ref[:n_earlier, 0] = pv_ref[:, 0]
        kt_ref[n_earlier, 0] = k.T
        vt_ref[n_earlier, 0] = v.T
    else:
        k_ref[...] = k

    z = jnp.dot(xn, wf_ref[...], preferred_element_type=F32) + fb_ref[...]
    lf = jnp.minimum(z, 0.0) - jnp.log1p(jnp.exp(-jnp.abs(z)))

    bs = min(CUM_BLOCK, tm)
    assert seq_len >= tm or bs % seq_len == 0
    row = lax.broadcasted_iota(jnp.int32, (bs, bs), 0)
    col = lax.broadcasted_iota(jnp.int32, (bs, bs), 1)
    same = col <= row
    if seq_len < tm:
        same = same & ((row // seq_len) == (col // seq_len))
    same = same.astype(BF16)
    offset = jnp.zeros((1, LANES), F32)
    if seq_len > tm:
        tiles_per_seq = seq_len // tm

        @pl.when(i % tiles_per_seq == 0)
        def _():
            carry_ref[...] = jnp.zeros_like(carry_ref)

        offset = carry_ref[...]
    blocks = []
    for r0 in range(0, tm, bs):
        blocks.append(_dot01_left(same, lf[r0:r0 + bs]) + offset)
        if seq_len >= tm:
            offset = blocks[-1][bs - 1:bs, :]
    cum = jnp.concatenate(blocks, axis=0) if len(blocks) > 1 else blocks[0]
    if seq_len > tm:
        carry_ref[...] = offset
    lf_ref[...] = lf[:, :H_FOX]
    cum_ref[...] = cum[:, :H_FOX]
    cumt_ref[0] = cum.T[:H_FOX, :]


def _inproj(x, g, w_all, w_f, f_bias, layer, seq_len, d_hg, d_fox, earlier=None, tm=512):
    n, d = x.shape
    tm = min(tm, n)
    assert seq_len % tm == 0 or tm % seq_len == 0
    d_main = d_hg + 3 * d_fox
    nt = n // tm
    tok = lambda w: pl.BlockSpec((tm, w), lambda i: (i, 0))
    tok_shape = lambda w: jax.ShapeDtypeStruct((n, w), F32)
    feature_major = seq_len % tm == 0
    n_earlier = 0 if earlier is None else earlier[0].shape[0]
    in_specs = [
        tok(d),
        _resident((1, d), lambda i: (0, 0)),
        _resident((None, d, d_main), lambda i: (layer, 0, 0)),
        _resident((None, d, LANES), lambda i: (layer, 0, 0)),
        _resident((None, 1, LANES), lambda i: (layer, 0, 0)),
    ]
    args = [x, g.reshape(1, d), w_all, w_f, f_bias]
    tail_specs = [tok(H_FOX), tok(H_FOX), pl.BlockSpec((1, H_FOX, tm), lambda i: (i, 0, 0))]
    tail_shape = [tok_shape(H_FOX), tok_shape(H_FOX), jax.ShapeDtypeStruct((nt, H_FOX, tm), F32)]
    if feature_major:
        tps = seq_len // tm
        stack = lambda layers: pl.BlockSpec((layers, 1, d_fox, tm), lambda i: (0, i // tps, 0, i % tps))
        if n_earlier:
            in_specs += [stack(n_earlier)] * 2
            args += list(earlier)
        out_specs = [tok(d_hg), tok(d_fox), tok(d_fox)] + tail_specs + [stack(n_earlier + 1)] * 2
        out_shape = ([tok_shape(d_hg), tok_shape(d_fox), tok_shape(d_fox)] + tail_shape
                     + [jax.ShapeDtypeStruct((n_earlier + 1, n // seq_len, d_fox, seq_len), F32)] * 2)
    else:
        assert earlier is None
        out_specs = [tok(d_hg), tok(d_fox), tok(d_fox), tok(d_fox)] + tail_specs
        out_shape = [tok_shape(d_hg), tok_shape(d_fox), tok_shape(d_fox), tok_shape(d_fox)] + tail_shape
    return pl.pallas_call(
        functools.partial(_inproj_kernel, seq_len=seq_len, d_hg=d_hg, d_fox=d_fox,
                          feature_major=feature_major, n_earlier=n_earlier),
        grid=(nt,),
        in_specs=in_specs,
        out_specs=out_specs,
        out_shape=out_shape,
        scratch_shapes=[pltpu.VMEM((1, LANES), F32)],
        compiler_params=_params("arbitrary"),
        name="inproj",
    )(*args)


def _hgrn_kernel(lbp_ref, aq_ref, af_ref, ai_ref, ag_ref, gain_ref, *rest,
                 layer, chunk, sub, valid_len, zero_init):
    if zero_init:
        o_ref, sout_ref, st_ref, ks_ref, as_ref, band_ref = rest
    else:
        s0_ref, o_ref, sout_ref, st_ref, ks_ref, as_ref, band_ref = rest
    tb = aq_ref.shape[1]
    nh = aq_ref.shape[2] // LANES
    nchunks = tb // chunk
    t = pl.program_id(1)

    @pl.when(t == 0)
    def _():
        for h in range(nh):
            if zero_init:
                st_ref[h] = jnp.zeros((LANES, LANES), F32)
            else:
                st_ref[h] = s0_ref[0, 0, h].T

    lbp = lbp_ref[...]
    e = jnp.exp(lbp - jnp.max(lbp, axis=0, keepdims=True))
    prob = e / jnp.sum(e, axis=0, keepdims=True)
    lb = jnp.zeros((1, nh * LANES), F32)
    for j in range(1, layer + 1):
        lb = lb + prob[j:j + 1, :]

    nsub = chunk // sub
    row = lax.broadcasted_iota(jnp.int32, (chunk, 1), 0)
    rmod = row % sub
    lane_off = lax.broadcasted_iota(jnp.int32, (chunk, LANES), 1) - (row - rmod)
    in_band = (lane_off >= 0) & (lane_off <= rmod)
    tri =(lax.broadcasted_iota(jnp.int32, (chunk, chunk), 1)
           <= lax.broadcasted_iota(jnp.int32, (chunk, chunk), 0)).astype(BF16)
    zpad_bf = jnp.zeros((LANES - chunk, LANES), BF16)
    gain_all = gain_ref[...]

    def sub_ends(a_all):
        return [a_all[(i + 1) * sub - 1:(i + 1) * sub, :] for i in range(nsub)]

    def prepare(c, slot):
        r0 = pl.multiple_of(c * chunk, chunk)
        rows = pl.ds(r0, chunk)
        f_all = lb + (1.0 - lb) * jax.nn.sigmoid(af_ref[0, rows, :])
        logf_all = jnp.log(jnp.maximum(f_all, TINY))
        k_all = 1.0 - f_all
        if valid_len is not None:
            live = (t * tb + r0 + row) < valid_len
            logf_all = jnp.where(live, logf_all, 0.0)
            k_all = jnp.where(live, k_all, 0.0)
        a_all = _dot01_left(tri, logf_all) * LOG2E
        ks_ref[slot] = k_all
        as_ref[slot] = a_all

        ends = sub_ends(a_all)
        span = -ends[0]
        for i in range(1, nsub):
            span = jnp.maximum(span, ends[i - 1] - ends[i])
        steep = jnp.max(span) > HGRN_SPAN_LIMIT

        ref_all = jnp.concatenate(
            [jnp.zeros((sub, nh * LANES), F32)]
            + [jnp.broadcast_to(ends[i - 1], (sub, nh * LANES)) for i in range(1, nsub)], axis=0)
        for h in range(nh):
            hs = slice(h * LANES, (h + 1) * LANES)
            a, ref = a_all[:, hs], ref_all[:, hs]
            qt = (aq_ref[0, rows, hs] * jnp.exp2(a - ref)).astype(BF16)
            kt = (k_all[:, hs] * jnp.exp2(jnp.minimum(ref - a, HGRN_SPAN_LIMIT))).astype(BF16)
            band = lax.dot_general(qt, jnp.concatenate([kt, zpad_bf], axis=0), NT_DIMS,
                                   preferred_element_type=F32)
            band_ref[slot, h] = jnp.where(in_band, band, 0.0)

        @pl.when(steep)
        def _():
            for h in range(nh):
                hs = slice(h * LANES, (h + 1) * LANES)
                q, a = aq_ref[0, rows, hs], a_all[:, hs]
                sc = jnp.zeros((chunk, LANES), F32)
                for j in range(sub):
                    kj = jnp.concatenate(
                        [jnp.broadcast_to(ks_ref[slot, i * sub + j:i * sub + j + 1, hs], (sub, LANES))
                         for i in range(nsub)], axis=0)
                    aj = jnp.concatenate(
                        [jnp.broadcast_to(as_ref[slot, i * sub + j:i * sub + j + 1, hs], (sub, LANES))
                         for i in range(nsub)], axis=0)
                    w = q * kj * jnp.exp2(a - aj)
                    col = jnp.where(rmod >= j, jnp.sum(w, axis=-1, keepdims=True), 0.0)
                    sc = jnp.where(lane_off == j, col, sc)
                band_ref[slot, h] = sc

    def chunk_body(c, carry):
        slot = c % 2
        r0 = pl.multiple_of(c * chunk, chunk)
        rows = pl.ds(r0, chunk)
        k_all = ks_ref[slot]
        a_all = as_ref[slot]
        ends = sub_ends(a_all)

        for h in range(nh):
            hs = slice(h * LANES, (h + 1) * LANES)
            q = aq_ref[0, rows, hs]
            v = ai_ref[0, rows, hs]
            g = ag_ref[0, rows, hs]
            k, a = k_all[:, hs], a_all[:, hs]
            st = st_ref[h]
            v_bf = v.astype(BF16)

            o = lax.dot_general((q * jnp.exp2(a)).astype(BF16), st.astype(BF16), NT_DIMS,
                                preferred_element_type=F32)

            blocks = [jnp.zeros((sub, LANES), F32)]
            for i in range(1, nsub):
                lo, hi = i * sub, (i + 1) * sub
                r = ends[i - 1][:, hs]
                qt = (q[lo:hi] * jnp.exp2(a[lo:hi] - r)).astype(BF16)
                kt = jnp.where(row < lo, k * jnp.exp2(r - a), 0.0).astype(BF16)
                blocks.append(lax.dot_general(qt, jnp.concatenate([kt, zpad_bf], axis=0), NT_DIMS,
                                              preferred_element_type=F32))
            sc = (jnp.concatenate(blocks, axis=0) if nsub > 1 else blocks[0]) + band_ref[slot, h]
            o = o + jnp.dot(sc.astype(BF16), jnp.concatenate([v_bf, zpad_bf], axis=0),
                            preferred_element_type=F32)

            a_last = a[chunk - 1:chunk, :]
            kt = (k * jnp.exp2(a_last - a)).astype(BF16)
            st_ref[h] = st * jnp.exp2(a_last) + lax.dot_general(
                v_bf, kt, TN_DIMS, preferred_element_type=F32)

            y = o * lax.rsqrt(jnp.mean(o * o, axis=-1, keepdims=True) + EPS) * gain_all[:, hs]
            y = y * (g * jax.nn.sigmoid(g))
            o_ref[0, rows, hs] = y.astype(o_ref.dtype)
        prepare(jnp.minimum(c + 1, nchunks - 1), 1 - slot)
        return carry

    prepare(0, 0)
    lax.fori_loop(0, nchunks, chunk_body, 0)

    @pl.when(t == pl.num_programs(1) - 1)
    def _():
        for h in range(nh):
            sout_ref[0, h] = st_ref[h].T


def _hgrn(hg, lb_param, gain, state0, layer, *, tb, chunk, sub, valid_len, out_dtype):
    bn, t_len, d4 = hg.shape
    dh = d4 // 4
    nh = dh // LANES
    depth = lb_param.shape[0]
    blk = lambda c: pl.BlockSpec((1, tb, dh), lambda b, t: (b, t, c))
    in_specs = [pl.BlockSpec((depth, dh), lambda b, t: (0, 0)),
                blk(0), blk(1), blk(2), blk(3),
                pl.BlockSpec((1, dh), lambda b, t: (0, 0))]
    args = [lb_param, hg, hg, hg, hg, gain.reshape(1, dh)]
    if state0 is not None:
        in_specs.append(pl.BlockSpec((1, 1, nh, LANES, LANES), lambda b, t: (layer, b, 0, 0, 0)))
        args.append(state0)
    return pl.pallas_call(
        functools.partial(_hgrn_kernel, layer=layer, chunk=chunk, sub=sub,
                          valid_len=valid_len, zero_init=state0 is None),
        grid=(bn, t_len // tb),
        in_specs=in_specs,
        out_specs=[pl.BlockSpec((1, tb, dh), lambda b, t: (b, t, 0)),
                   pl.BlockSpec((1, nh, LANES, LANES), lambda b, t: (b, 0, 0, 0))],
        out_shape=[jax.ShapeDtypeStruct((bn, t_len, dh), out_dtype),
                   jax.ShapeDtypeStruct((bn, nh, LANES, LANES), F32)],
        scratch_shapes=[pltpu.VMEM((nh, LANES, LANES), F32),
                        pltpu.VMEM((2, chunk, dh), F32),
                        pltpu.VMEM((2, chunk, dh), F32),
                        pltpu.VMEM((2, nh, chunk, LANES), F32)],
        compiler_params=_params("parallel", "arbitrary"),
        name="hgrn",
    )(*args)


def _fox_prompt_kernel(q_ref, kt_ref, v_ref, cum_ref, cumt_ref, gain_ref, o_ref,
                       s_ref, p_ref, fill_ref, m_ref, l_ref, acc_ref, *, scale):
    tq = q_ref.shape[1]
    rc = FOX_ROW_CHUNK
    band = max(tq // FOX_DIAG_BANDS, LANES)
    assert tq % band == 0 and band % rc == 0 and band % LANES == 0
    dh = LANES // 2
    hp = pl.program_id(1)
    qi = pl.program_id(2)
    lane = lax.broadcasted_iota(jnp.int32, (1, LANES), 1)
    q = q_ref[0] * (scale * LOG2E)
    cum = cum_ref[0]
    hl = lax.broadcasted_iota(jnp.int32, (1, H_FOX), 1)

    heads = range(2)
    qms = []
    for hh in heads:
        qms.append(jnp.where((lane // dh) == hh, q, 0.0).astype(BF16))
        cq = jnp.sum(jnp.where(hl == 2 * hp + hh, cum, 0.0), axis=-1, keepdims=True)
        fill_ref[hh] = jnp.broadcast_to((NEG - cq) * LOG2E, (tq, LANES))
        m_ref[hh] = jnp.full((tq, LANES), -jnp.inf, F32)
        l_ref[hh] = jnp.zeros((tq, LANES), F32)
        acc_ref[hh] = jnp.zeros((tq, LANES), F32)

    def key_bias(hh, kb):
        k0 = pl.multiple_of(kb * tq, tq)
        return cumt_ref[0, pl.ds(2 * hp + hh, 1), pl.ds(k0, tq)] * LOG2E

    def scores(kb, masked):
        k0 = pl.multiple_of(kb * tq, tq)
        kt = kt_ref[0, :, pl.ds(k0, tq)].astype(BF16)
        for hh in heads:
            if masked:
                for r1 in range(band, tq + 1, band):
                    s_ref[hh, kb, r1 - band:r1, :r1] = jnp.dot(
                        qms[hh][r1 - band:r1], kt[:, :r1], preferred_element_type=F32)
            else:
                s_ref[hh, kb] = jnp.dot(qms[hh], kt, preferred_element_type=F32)
        for hh in heads:
            ck = key_bias(hh, kb)
            for r in range(tq // rc):
                rs = slice(r * rc, (r + 1) * rc)
                m = m_ref[hh, rs, :]
                for j in range(tq // LANES):
                    cs = slice(j * LANES, (j + 1) * LANES)
                    if masked and j * LANES > (r + 1) * rc - 1:
                        continue
                    s = s_ref[hh, kb, rs, cs] - ck[:, cs]
                    if masked and (j + 1) * LANES - 1 > r * rc:
                        rid = r * rc + lax.broadcasted_iota(jnp.int32, (rc, LANES), 0)
                        cid = j * LANES + lax.broadcasted_iota(jnp.int32, (rc, LANES), 1)
                        s = jnp.where(rid >= cid, s, fill_ref[hh, rs, :])
                    if masked:
                        s_ref[hh, kb, rs, cs] = s
                    m = jnp.maximum(m, s)
                m_ref[hh, rs, :] = m

    def weights(kb, masked):
        k0 = pl.multiple_of(kb * tq, tq)
        v = v_ref[0, pl.ds(k0, tq), :].astype(BF16)
        for hh in heads:
            ck = key_bias(hh, kb)
            for r in range(tq // rc):
                rs = slice(r * rc, (r + 1) * rc)
                m = m_ref[hh, rs, :]
                lsum = l_ref[hh, rs, :]
                for j in range(tq // LANES):
                    cs = slice(j * LANES, (j + 1) * LANES)
                    if masked and j * LANES > (r + 1) * rc - 1:
                        if j * LANES < -(-(r + 1) * rc // band) * band:
                            p_ref[hh, rs, cs] = jnp.zeros((rc, LANES), BF16)
                        continue
                    s = s_ref[hh, kb, rs, cs]
                    if not masked:
                        s = s - ck[:, cs]
                    p = jnp.exp2(s - m)
                    lsum = lsum + p
                    p_ref[hh, rs, cs] = p.astype(BF16)
                l_ref[hh, rs, :] = lsum
            if masked:
                for r1 in range(band, tq + 1, band):
                    acc_ref[hh, r1 - band:r1, :] += jnp.dot(p_ref[hh, r1 - band:r1, :r1], v[:r1],
                                                            preferred_element_type=F32)
            else:
                acc_ref[hh] += jnp.dot(p_ref[hh], v, preferred_element_type=F32)

    def loop(fn):
        def body(kb, carry):
            fn(kb, masked=False)
            return carry
        lax.fori_loop(0, qi, body, 0)
        fn(qi, masked=True)

    loop(scores)
    for hh in heads:
        m_ref[hh] = jnp.broadcast_to(jnp.max(m_ref[hh], axis=-1, keepdims=True), (tq, LANES))
    loop(weights)

    low = (lane // dh) == 0
    o = jnp.where(low, acc_ref[0] / jnp.sum(l_ref[0], axis=-1, keepdims=True),
                  acc_ref[1] / jnp.sum(l_ref[1], axis=-1, keepdims=True))
    sq = o * o
    ss0 = jnp.sum(jnp.where(low, sq, 0.0), axis=-1, keepdims=True)
    ss1 = jnp.sum(jnp.where(low, 0.0, sq), axis=-1, keepdims=True)
    inv = jnp.where(low, lax.rsqrt(ss0 / dh + EPS), lax.rsqrt(ss1 / dh + EPS))
    o_ref[0] = (o * inv * gain_ref[...]).astype(o_ref.dtype)


def _fox_prompt(q, kt_stack, layer, v, cum, cumt, gain, tq=1024):
    bn, t_len, d = q.shape
    tq = min(tq, t_len)
    nhp = d // LANES
    scale = float((LANES // 2) ** -0.5)
    return pl.pallas_call(
        functools.partial(_fox_prompt_kernel, scale=scale),
        grid=(bn, nhp, t_len // tq),
        in_specs=[
            pl.BlockSpec((1, tq, LANES), lambda b, h, i: (b, i, h)),
            pl.BlockSpec((None, 1, LANES, t_len), lambda b, h, i: (layer, b, h, 0)),
            pl.BlockSpec((1, t_len, LANES), lambda b, h, i: (b, 0, h)),
            pl.BlockSpec((1, tq, H_FOX), lambda b, h, i: (b, i, 0)),
            pl.BlockSpec((1, H_FOX, t_len), lambda b, h, i: (b, 0, 0)),
            pl.BlockSpec((1, LANES), lambda b, h, i: (0, h)),
        ],
        out_specs=pl.BlockSpec((1, tq, LANES), lambda b, h, i: (b, i, h)),
        out_shape=jax.ShapeDtypeStruct((bn, t_len, d), BF16),
        scratch_shapes=[pltpu.VMEM((2, t_len // tq, tq, tq), F32),
                        pltpu.VMEM((2, tq, tq), BF16),
                        pltpu.VMEM((2, tq, LANES), F32),
                        pltpu.VMEM((2, tq, LANES), F32),
                        pltpu.VMEM((2, tq, LANES), F32),
                        pltpu.VMEM((2, tq, LANES), F32)],
        compiler_params=_params("parallel", "parallel", "arbitrary"),
        name="fox_prompt",
    )(q, kt_stack, v, cum, cumt, gain.reshape(1, d))


def _suffix_kernel(lf_ref, tab_ref):
    tp, nh, n = lf_ref.shape
    i = lax.broadcasted_iota(jnp.int32, (n, n), 0)
    j = lax.broadcasted_iota(jnp.int32, (n, n), 1)
    lf = lf_ref[...].reshape(tp * nh, n)
    suf = _dot01_right(lf, (i > j).astype(BF16))
    tot = _dot01_right(lf, jnp.ones((n, n), BF16))
    tab_ref[:, :nh, :] = suf.reshape(tp, nh, n)
    tab_ref[:, nh:, :] = tot.reshape(tp, nh, n)


def _suffix(lf_t, tp=512):
    pages, nh, n = lf_t.shape
    tp = min(tp, pages)
    assert pages % tp == 0
    return pl.pallas_call(
        _suffix_kernel,
        grid=(pages // tp,),
        in_specs=[pl.BlockSpec((tp, nh, n), lambda i: (i, 0, 0))],
        out_specs=pl.BlockSpec((tp, 2 * nh, n), lambda i: (i, 0, 0)),
        out_shape=jax.ShapeDtypeStruct((pages, 2 * nh, n), F32),
        compiler_params=_params("parallel"),
        name="suffix",
    )(lf_t)


def _fox_decode_kernel(pt_ref, q_ref, kn_ref, vn_ref, cq_ref, gain_ref, tab_ref, ck_hbm, cv_hbm,
                       o_ref, m_ref, l_ref, acc_ref, carry_ref, qbd_ref, kbuf, vbuf, sem,
                       *, pps, scale, first_page):
    steps = pl.num_programs(1)
    n_pages = steps * pps
    b = pl.program_id(0)
    j = pl.program_id(1)
    t_new = q_ref.shape[1]
    d = q_ref.shape[2]
    dh = d // H_FOX
    rows = t_new * H_FOX
    page = kbuf.shape[3]

    g = b * steps + j
    total = pl.num_programs(0) * steps

    def page_copies(gg, slot):
        bb = gg // steps
        jj = gg - bb * steps
        copies = []
        for i in range(pps):
            pid = first_page + pt_ref[bb, n_pages - 1 - (jj * pps + i)]
            copies.append(pltpu.make_async_copy(ck_hbm.at[pid], kbuf.at[slot, i], sem.at[slot, 0, i]))
            copies.append(pltpu.make_async_copy(cv_hbm.at[pid], vbuf.at[slot, i], sem.at[slot, 1, i]))
        return copies

    @pl.when(g == 0)
    def _():
        for ahead in range(DECODE_SLOTS - 1):
            @pl.when(ahead < total)
            def _():
                for c in page_copies(ahead, ahead):
                    c.start()

    @pl.when(g + DECODE_SLOTS - 1 < total)
    def _():
        nxt = g + DECODE_SLOTS - 1
        for c in page_copies(nxt, nxt % DECODE_SLOTS):
            c.start()

    slot = g % DECODE_SLOTS
    for c in page_copies(g, slot):
        c.wait()
    k_refs = [kbuf.at[slot, i] for i in range(pps)]
    v_refs = [vbuf.at[slot, i] for i in range(pps)]
    head_of_lane = lax.broadcasted_iota(jnp.int32, (H_FOX, d), 1) // dh
    head_of_row = lax.broadcasted_iota(jnp.int32, (H_FOX, d), 0)
    diag = head_of_lane == head_of_row
    cq = cq_ref[0]

    @pl.when(j == 0)
    def _():
        qbd = jnp.concatenate(
            [jnp.where(diag, q_ref[0, t:t + 1, :] * scale, 0.0) for t in range(t_new)], axis=0)
        qbd_ref[...] = qbd
        qrow = lax.broadcasted_iota(jnp.int32, (rows, 1), 0) // H_FOX
        s_new = []
        for s in range(t_new):
            dot_s = jnp.sum(qbd * kn_ref[0, s:s + 1, :], axis=-1, keepdims=True)
            cs = jnp.concatenate([cq[s * H_FOX:(s + 1) * H_FOX]] * t_new, axis=0)
            s_new.append(jnp.where(qrow >= s, dot_s + cq - cs, NEG))
        m0 = s_new[0]
        for s in range(1, t_new):
            m0 = jnp.maximum(m0, s_new[s])
        l0 = jnp.zeros((rows, 1), F32)
        acc0 = jnp.zeros((rows, d), F32)
        for s in range(t_new):
            p_s = jnp.exp(s_new[s] - m0)
            l0 = l0 + p_s
            acc0 = acc0 + p_s * vn_ref[0, s:s + 1, :]
        m_ref[...] = jnp.broadcast_to(m0, (rows, LANES))
        l_ref[...] = jnp.broadcast_to(l0, (rows, LANES))
        acc_ref[...] = acc0
        carry_ref[...] = jnp.zeros_like(carry_ref)

    qbd = qbd_ref[...].astype(BF16)
    carry = carry_ref[...]
    groups = [range(g0, min(g0 + DECODE_GROUP, pps)) for g0 in range(0, pps, DECODE_GROUP)]
    scores = []
    for group in groups:
        s_parts = []
        for i in group:
            kpg = k_refs[i][...].astype(BF16)
            s = jnp.dot(qbd, kpg, preferred_element_type=F32)
            pid = pt_ref[b, n_pages - 1 - (j * pps + i)]
            suf = tab_ref[pid, :H_FOX, :] + carry
            carry = carry + tab_ref[pid, H_FOX:, :]
            s_parts.append(s + jnp.concatenate([suf] * t_new, axis=0) + cq)
        scores.append(jnp.concatenate(s_parts, axis=1))
    carry_ref[...] = carry
    softmaxes = []
    for s_g in scores:
        m_g = jnp.max(s_g, axis=-1, keepdims=True)
        p = jnp.exp(s_g - m_g)
        softmaxes.append((m_g, jnp.sum(p, axis=-1, keepdims=True), p.astype(BF16)))
    partials = []
    for group, (m_g, l_g, p_bf) in zip(groups, softmaxes):
        vt = jnp.concatenate([v_refs[i][...] for i in group], axis=1).astype(BF16)
        partials.append((m_g, l_g, lax.dot_general(p_bf, vt, NT_DIMS, preferred_element_type=F32)))
    m_prev = m_ref[...][:, :1]
    m_new = m_prev
    for m_g, _, _ in partials:
        m_new = jnp.maximum(m_new, m_g)
    alpha = jnp.exp(m_prev - m_new)
    l_new = alpha * l_ref[...][:, :1]
    acc = alpha * acc_ref[...]
    for m_g, l_g, acc_g in partials:
        w_g = jnp.exp(m_g - m_new)
        l_new = l_new + w_g * l_g
        acc = acc + w_g * acc_g
    m_ref[...] = jnp.broadcast_to(m_new, m_ref.shape)
    l_ref[...] = jnp.broadcast_to(l_new, l_ref.shape)
    acc_ref[...] = acc

    @pl.when(j == pl.num_programs(1) - 1)
    def _():
        o = acc_ref[...] / l_ref[...][:, :1]
        o = jnp.where(jnp.concatenate([diag] * t_new, axis=0), o, 0.0)
        y = o * lax.rsqrt(jnp.sum(o * o, axis=-1, keepdims=True) / dh + EPS)
        out = jnp.concatenate(
            [jnp.sum(y[t * H_FOX:(t + 1) * H_FOX], axis=0, keepdims=True) for t in range(t_new)],
            axis=0)
        o_ref[0] = out * gain_ref[...]


def _fox_decode(q, k_new, v_new, cq_col, gain, cache_k, cache_v, suffix_tab, page_table,
                layer, n_pool, pps=8):
    bn, t_new, d = q.shape
    page = cache_k.shape[2]
    n_pages = page_table.shape[1]
    pps = min(pps, n_pages)
    assert n_pages % pps == 0
    rows = t_new * H_FOX
    scale = float((d // H_FOX) ** -0.5)
    tok = pl.BlockSpec((1, t_new, d), lambda b, j, pt: (b, 0, 0))
    in_specs = [tok, tok, tok,
                pl.BlockSpec((1, rows, 1), lambda b, j, pt: (b, 0, 0)),
                pl.BlockSpec((1, d), lambda b, j, pt: (0, 0)),
                _resident((None, n_pool, 2 * H_FOX, page), lambda b, j, pt: (layer, 0, 0, 0)),
                pl.BlockSpec(memory_space=pl.ANY),
                pl.BlockSpec(memory_space=pl.ANY)]
    grid_spec = pltpu.PrefetchScalarGridSpec(
        num_scalar_prefetch=1,
        grid=(bn, n_pages // pps),
        in_specs=in_specs,
        out_specs=pl.BlockSpec((1, t_new, d), lambda b, j, pt: (b, 0, 0)),
        scratch_shapes=[pltpu.VMEM((rows, LANES), F32),
                        pltpu.VMEM((rows, LANES), F32),
                        pltpu.VMEM((rows, d), F32),
                        pltpu.VMEM((H_FOX, page), F32),
                        pltpu.VMEM((rows, d), F32),
                        pltpu.VMEM((DECODE_SLOTS, pps, d, page), F32),
                        pltpu.VMEM((DECODE_SLOTS, pps, d, page), F32),
                        pltpu.SemaphoreType.DMA((DECODE_SLOTS, 2, pps))],
    )
    return pl.pallas_call(
        functools.partial(_fox_decode_kernel, pps=pps, scale=scale, first_page=layer * n_pool),
        grid_spec=grid_spec,
        out_shape=jax.ShapeDtypeStruct((bn, t_new, d), F32),
        compiler_params=_params("arbitrary", "arbitrary"),
        name="fox_decode",
    )(page_table, q, k_new, v_new, cq_col, gain.reshape(1, d), suffix_tab, cache_k, cache_v)


def kernel(x_prompt, x_sample, cache_k, cache_v, cache_logf, state_hgrn, page_table, norm_ffn1, ffn1_w_in, ffn1_w_out, norm_mix, w_in_mix, hgrn_lb, fox_f_bias, hgrn_out_gain, fox_out_gain, w_out_mix, norm_ffn2, ffn2_w_in, ffn2_w_out, norm_final):
    depth = norm_ffn1.shape[0]
    bn, t_len, d = x_prompt.shape
    db, t_new, _ = x_sample.shape
    d_fox = fox_out_gain.shape[1]
    d_h = hgrn_out_gain.shape[1]
    dh_fox = d_fox // H_FOX
    n_pool, page = cache_k.shape[1], cache_k.shape[2]
    d_main = 4 * d_h + 3 * d_fox

    w1_in, w1_out = ffn1_w_in.astype(BF16), ffn1_w_out.astype(BF16)
    w2_in, w2_out = ffn2_w_in.astype(BF16), ffn2_w_out.astype(BF16)
    w_mix = w_in_mix.astype(BF16)
    w_f = jnp.pad(w_in_mix[:, :, d_main:], ((0, 0), (0, 0), (0, LANES - H_FOX))).astype(BF16)
    f_bias = jnp.pad(fox_f_bias, ((0, 0), (0, LANES - H_FOX))).reshape(depth, 1, LANES)
    w_out = w_out_mix.astype(BF16)

    ck = jnp.transpose(cache_k, (0, 1, 3, 4, 2)).reshape(depth * n_pool, d_fox, page)
    cv = jnp.transpose(cache_v, (0, 1, 3, 4, 2)).reshape(depth * n_pool, d_fox, page)
    lf_t = jnp.swapaxes(cache_logf, 2, 3).reshape(depth * n_pool, H_FOX, page)
    suffix_tab = _suffix(lf_t).reshape(depth, n_pool, 2 * H_FOX, page)

    bf16_rows = 2 * SUBLANES
    t_pad = -(-t_new // bf16_rows) * bf16_rows

    def run(x, seq_len, nseq, prompt):
        ks, vs, lfs, ss = [], [], [], []
        kv_stack = None
        for l in range(depth):
            x = _ffn(x, norm_ffn1[l], w1_in, w1_out, l)
            if prompt:
                hg, q, v, lf, cum, cumt, *kv_stack = _inproj(
                    x, norm_mix[l], w_mix, w_f, f_bias, l, seq_len, 4 * d_h, d_fox,
                    earlier=kv_stack, tm=min(512, seq_len))
                o_a, s_fin = _hgrn(hg.reshape(nseq, seq_len, 4 * d_h), hgrn_lb, hgrn_out_gain[l], None, l,
                                   tb=min(512, seq_len), chunk=64, sub=16, valid_len=None, out_dtype=BF16)
                o_a = o_a.reshape(nseq * seq_len, d_h)
                tiles = cumt.shape[0] // nseq
                cumt_b = cumt.reshape(nseq, tiles, H_FOX, -1).transpose(0, 2, 1, 3).reshape(nseq, H_FOX, seq_len)
                o_f = _fox_prompt(q.reshape(nseq, seq_len, d_fox), kv_stack[0], l,
                                  v.reshape(nseq, seq_len, d_fox), cum.reshape(nseq, seq_len, H_FOX),
                                  cumt_b, fox_out_gain[l]).reshape(nseq * seq_len, d_fox)
            else:
                hg, q, k, v, lf, cum, cumt = _inproj(x, norm_mix[l], w_mix, w_f, f_bias, l, seq_len,
                                                     4 * d_h, d_fox)
                ks.append(k.reshape(nseq, seq_len, H_FOX, dh_fox))
                vs.append(v.reshape(nseq, seq_len, H_FOX, dh_fox))
                hg_p = jnp.pad(hg.reshape(nseq, seq_len, 4 * d_h), ((0, 0), (0, t_pad - seq_len), (0, 0)))
                o_a, s_fin = _hgrn(hg_p, hgrn_lb, hgrn_out_gain[l], state_hgrn, l,
                                   tb=t_pad, chunk=t_pad, sub=t_pad, valid_len=seq_len, out_dtype=F32)
                o_a = o_a[:, :seq_len].reshape(nseq * seq_len, d_h).astype(BF16)
                o_f = _fox_decode(q.reshape(nseq, seq_len, d_fox), k.reshape(nseq, seq_len, d_fox),
                                  v.reshape(nseq, seq_len, d_fox), cum.reshape(nseq, seq_len * H_FOX, 1),
                                  fox_out_gain[l], ck, cv, suffix_tab, page_table, l, n_pool)
                o_f = o_f.reshape(nseq * seq_len, d_fox).astype(BF16)
            x = _ffn(x, norm_ffn2[l], w2_in, w2_out, l, mixer=(o_a, o_f, w_out),
                     final_g=norm_final if l == depth - 1 else None)
            lfs.append(lf.reshape(nseq, seq_len, H_FOX))
            ss.append(s_fin)
        if prompt:
            k_all, v_all = (a.reshape(depth, nseq, H_FOX, dh_fox, seq_len).transpose(0, 1, 4, 2, 3)
                            for a in kv_stack)
        else:
            k_all, v_all = jnp.stack(ks), jnp.stack(vs)
        return (x.reshape(nseq, seq_len, d), k_all, v_all, jnp.stack(lfs), jnp.stack(ss))

    y_p, k_p, v_p, lf_p, s_p = run(x_prompt.reshape(bn * t_len, d), t_len, bn, True)
    y_s, k_s, v_s, lf_s, s_s = run(x_sample.reshape(db * t_new, d), t_new, db, False)
    return (y_p, y_s, k_p, v_p, lf_p, s_p, k_s, v_s, lf_s, s_s)
```

```python
import functools

import jax
import jax.numpy as jnp
from jax import lax
from jax.experimental import pallas as pl
from jax.experimental.pallas import tpu as pltpu

F32 = jnp.float32
BF16 = jnp.bfloat16

EPS = 1e-6
TINY = 1e-30
NEG = -1e30

LOG2E = 1.4426950408889634
H_HGRN = 4
H_FOX = 8
FOX_ROW_CHUNK = 64
FOX_DIAG_BANDS = 4
CUM_BLOCK = 256
HGRN_SPAN_LIMIT = 100.0
DECODE_GROUP = 2
DECODE_SLOTS = 3
LANES = 128
SUBLANES = 8
VMEM_LIMIT = 56 * 1024 * 1024

NT_DIMS = (((1,), (1,)), ((), ()))
TN_DIMS = (((0,), (0,)), ((), ()))


def _params(*sem):
    return pltpu.CompilerParams(dimension_semantics=sem, vmem_limit_bytes=VMEM_LIMIT)


def _resident(shape, index_map):
    return pl.BlockSpec(shape, index_map, pipeline_mode=pl.Buffered(1))


def _rms(x, g):
    return x * lax.rsqrt(jnp.mean(x * x, axis=-1, keepdims=True) + EPS) * g


def _split3(x):
    hi = x.astype(BF16)
    r = x - hi.astype(F32)
    mid = r.astype(BF16)
    lo = (r - mid.astype(F32)).astype(BF16)
    return hi, mid, lo


def _dot01_left(m01, x):
    return sum(jnp.dot(m01, t, preferred_element_type=F32) for t in _split3(x))


def _dot01_right(x, m01):
    return sum(jnp.dot(t, m01, preferred_element_type=F32) for t in _split3(x))


def _ffn_kernel(x_ref, g_ref, wa_ref, wb_ref, wo_ref, *rest, final_norm, mixer_out):
    rest = list(rest)
    o_ref = rest.pop()
    x = x_ref[...]
    if mixer_out:
        oa_ref, of_ref, wma_ref, wmf_ref = rest[:4]
        del rest[:4]
        x = (x + jnp.dot(oa_ref[...], wma_ref[...], preferred_element_type=F32)
             + jnp.dot(of_ref[...], wmf_ref[...], preferred_element_type=F32))
    if final_norm:
        (gf_ref,) = rest
    xn = _rms(x, g_ref[...]).astype(BF16)
    a = jnp.dot(xn, wa_ref[...], preferred_element_type=F32)
    b = jnp.dot(xn, wb_ref[...], preferred_element_type=F32)
    h = (a * jax.nn.sigmoid(a) * b).astype(BF16)
    y = x + 0.5 * jnp.dot(h, wo_ref[...], preferred_element_type=F32)
    if final_norm:
        y = _rms(y, gf_ref[...])
    o_ref[...] = y


def _ffn(x, g, w_in, w_out, layer, mixer=None, final_g=None, tm=512):
    n, d = x.shape
    ff = w_out.shape[1]
    tm = min(tm, n)
    in_specs = [
        pl.BlockSpec((tm, d), lambda i: (i, 0)),
        _resident((1, d), lambda i: (0, 0)),
        _resident((None, d, ff), lambda i: (layer, 0, 0)),
        _resident((None, d, ff), lambda i: (layer, 0, 1)),
        _resident((None, ff, d), lambda i: (layer, 0, 0)),
    ]
    args = [x, g.reshape(1, d), w_in, w_in, w_out]
    if mixer is not None:
        o_a, o_f, w_mix_out = mixer
        dm = o_a.shape[1]
        assert o_f.shape[1] == dm and w_mix_out.shape[1] == 2 * dm
        in_specs += [pl.BlockSpec((tm, dm), lambda i: (i, 0)),
                     pl.BlockSpec((tm, dm), lambda i: (i, 0)),
                     _resident((None, dm, d), lambda i: (layer, 0, 0)),
                     _resident((None, dm, d), lambda i: (layer, 1, 0))]
        args += [o_a, o_f, w_mix_out, w_mix_out]
    if final_g is not None:
        in_specs.append(_resident((1, d), lambda i: (0, 0)))
        args.append(final_g.reshape(1, d))
    return pl.pallas_call(
        functools.partial(_ffn_kernel, final_norm=final_g is not None, mixer_out=mixer is not None),
        grid=(n // tm,),
        in_specs=in_specs,
        out_specs=pl.BlockSpec((tm, d), lambda i: (i, 0)),
        out_shape=jax.ShapeDtypeStruct((n, d), F32),
        compiler_params=_params("parallel"),
        name="ffn",
    )(*args)


def _inproj_kernel(x_ref, g_ref, w_ref, wf_ref, fb_ref, *rest,
                   seq_len, d_hg, d_fox, feature_major, n_earlier):
    rest = list(rest)
    carry_ref = rest.pop()
    if n_earlier:
        pk_ref, pv_ref = rest[:2]
        del rest[:2]
    if feature_major:
        hg_ref, q_ref, v_ref, lf_ref, cum_ref, cumt_ref, kt_ref, vt_ref = rest
    else:
        hg_ref, q_ref, k_ref, v_ref, lf_ref, cum_ref, cumt_ref = rest
    tm = x_ref.shape[0]
    i = pl.program_id(0)
    xn = _rms(x_ref[...], g_ref[...]).astype(BF16)
    p = jnp.dot(xn, w_ref[...], preferred_element_type=F32)
    hg_ref[...] = p[:, :d_hg]
    q_ref[...] = p[:, d_hg:d_hg + d_fox]
    k = p[:, d_hg + d_fox:d_hg + 2 * d_fox]
    v = p[:, d_hg + 2 * d_fox:d_hg + 3 * d_fox]
    v_ref[...] = v
    if feature_major:
        if n_earlier:
            kt_ref[:n_earlier, 0] = pk_ref[:, 0]
            vt_ref[:n_earlier, 0] = pv_ref[:, 0]
        kt_ref[n_earlier, 0] = k.T
        vt_ref[n_earlier, 0] = v.T
    else:
        k_ref[...] = k

    z = jnp.dot(xn, wf_ref[...], preferred_element_type=F32) + fb_ref[...]
    lf = jnp.minimum(z, 0.0) - jnp.log1p(jnp.exp(-jnp.abs(z)))

    bs = min(CUM_BLOCK, tm)
    assert seq_len >= tm or bs % seq_len == 0
    row = lax.broadcasted_iota(jnp.int32, (bs, bs), 0)
    col = lax.broadcasted_iota(jnp.int32, (bs, bs), 1)
    same = col <= row
    if seq_len < tm:
        same = same & ((row // seq_len) == (col // seq_len))
    same = same.astype(BF16)
    offset = jnp.zeros((1, LANES), F32)
    if seq_len > tm:
        tiles_per_seq = seq_len // tm

        @pl.when(i % tiles_per_seq == 0)
        def _():
            carry_ref[...] = jnp.zeros_like(carry_ref)

        offset = carry_ref[...]
    blocks = []
    for r0 in range(0, tm, bs):
        blocks.append(_dot01_left(same, lf[r0:r0 + bs]) + offset)
        if seq_len >= tm:
            offset = blocks[-1][bs - 1:bs, :]
    cum = jnp.concatenate(blocks, axis=0) if len(blocks) > 1 else blocks[0]
    if seq_len > tm:
        carry_ref[...] = offset
    lf_ref[...] = lf[:, :H_FOX]
    cum_ref[...] = cum[:, :H_FOX]
    cumt_ref[0] = cum.T[:H_FOX, :]


def _inproj(x, g, w_all, w_f, f_bias, layer, seq_len, d_hg, d_fox, earlier=None, tm=512):
    n, d = x.shape
    tm = min(tm, n)
    assert seq_len % tm == 0 or tm % seq_len == 0
    d_main = d_hg + 3 * d_fox
    nt = n // tm
    tok = lambda w: pl.BlockSpec((tm, w), lambda i: (i, 0))
    tok_shape = lambda w: jax.ShapeDtypeStruct((n, w), F32)
    feature_major = seq_len % tm == 0
    n_earlier = 0 if earlier is None else earlier[0].shape[0]
    in_specs = [
        tok(d),
        _resident((1, d), lambda i: (0, 0)),
        _resident((None, d, d_main), lambda i: (layer, 0, 0)),
        _resident((None, d, LANES), lambda i: (layer, 0, 0)),
        _resident((None, 1, LANES), lambda i: (layer, 0, 0)),
    ]
    args = [x, g.reshape(1, d), w_all, w_f, f_bias]
    tail_specs = [tok(H_FOX), tok(H_FOX), pl.BlockSpec((1, H_FOX, tm), lambda i: (i, 0, 0))]
    tail_shape = [tok_shape(H_FOX), tok_shape(H_FOX), jax.ShapeDtypeStruct((nt, H_FOX, tm), F32)]
    if feature_major:
        tps = seq_len // tm
        stack = lambda layers: pl.BlockSpec((layers, 1, d_fox, tm), lambda i: (0, i // tps, 0, i % tps))
        if n_earlier:
            in_specs += [stack(n_earlier)] * 2
            args += list(earlier)
        out_specs = [tok(d_hg), tok(d_fox), tok(d_fox)] + tail_specs + [stack(n_earlier + 1)] * 2
        out_shape = ([tok_shape(d_hg), tok_shape(d_fox), tok_shape(d_fox)] + tail_shape
                     + [jax.ShapeDtypeStruct((n_earlier + 1, n // seq_len, d_fox, seq_len), F32)] * 2)
    else:
        assert earlier is None
        out_specs = [tok(d_hg), tok(d_fox), tok(d_fox), tok(d_fox)] + tail_specs
        out_shape = [tok_shape(d_hg), tok_shape(d_fox), tok_shape(d_fox), tok_shape(d_fox)] + tail_shape
    return pl.pallas_call(
        functools.partial(_inproj_kernel, seq_len=seq_len, d_hg=d_hg, d_fox=d_fox,
                          feature_major=feature_major, n_earlier=n_earlier),
        grid=(nt,),
        in_specs=in_specs,
        out_specs=out_specs,
        out_shape=out_shape,
        scratch_shapes=[pltpu.VMEM((1, LANES), F32)],
        compiler_params=_params("arbitrary"),
        name="inproj",
    )(*args)


def _hgrn_kernel(lbp_ref, aq_ref, af_ref, ai_ref, ag_ref, gain_ref, *rest,
                 layer, chunk, sub, valid_len, zero_init):
    if zero_init:
        o_ref, sout_ref, st_ref, ks_ref, as_ref, band_ref = rest
    else:
        s0_ref, o_ref, sout_ref, st_ref, ks_ref, as_ref, band_ref = rest
    tb = aq_ref.shape[1]
    nh = aq_ref.shape[2] // LANES
    nchunks = tb // chunk
    t = pl.program_id(1)

    @pl.when(t == 0)
    def _():
        for h in range(nh):
            if zero_init:
                st_ref[h] = jnp.zeros((LANES, LANES), F32)
            else:
                st_ref[h] = s0_ref[0, 0, h].T

    lbp = lbp_ref[...]
    e = jnp.exp(lbp - jnp.max(lbp, axis=0, keepdims=True))
    prob = e / jnp.sum(e, axis=0, keepdims=True)
    lb = jnp.zeros((1, nh * LANES), F32)
    for j in range(1, layer + 1):
        lb = lb + prob[j:j + 1, :]

    nsub = chunk // sub
    row = lax.broadcasted_iota(jnp.int32, (chunk, 1), 0)
    rmod = row % sub
    lane_off = lax.broadcasted_iota(jnp.int32, (chunk, LANES), 1) - (row - rmod)
    in_band = (lane_off >= 0) & (lane_off <= rmod)
    tri =(lax.broadcasted_iota(jnp.int32, (chunk, chunk), 1)
           <= lax.broadcasted_iota(jnp.int32, (chunk, chunk), 0)).astype(BF16)
    zpad_bf = jnp.zeros((LANES - chunk, LANES), BF16)
    gain_all = gain_ref[...]

    def sub_ends(a_all):
        return [a_all[(i + 1) * sub - 1:(i + 1) * sub, :] for i in range(nsub)]

    def prepare(c, slot):
        r0 = pl.multiple_of(c * chunk, chunk)
        rows = pl.ds(r0, chunk)
        f_all = lb + (1.0 - lb) * jax.nn.sigmoid(af_ref[0, rows, :])
        logf_all = jnp.log(jnp.maximum(f_all, TINY))
        k_all = 1.0 - f_all
        if valid_len is not None:
            live = (t * tb + r0 + row) < valid_len
            logf_all = jnp.where(live, logf_all, 0.0)
            k_all = jnp.where(live, k_all, 0.0)
        a_all = _dot01_left(tri, logf_all) * LOG2E
        ks_ref[slot] = k_all
        as_ref[slot] = a_all

        ends = sub_ends(a_all)
        span = -ends[0]
        for i in range(1, nsub):
            span = jnp.maximum(span, ends[i - 1] - ends[i])
        steep = jnp.max(span) > HGRN_SPAN_LIMIT

        ref_all = jnp.concatenate(
            [jnp.zeros((sub, nh * LANES), F32)]
            + [jnp.broadcast_to(ends[i - 1], (sub, nh * LANES)) for i in range(1, nsub)], axis=0)
        for h in range(nh):
            hs = slice(h * LANES, (h + 1) * LANES)
            a, ref = a_all[:, hs], ref_all[:, hs]
            qt = (aq_ref[0, rows, hs] * jnp.exp2(a - ref)).astype(BF16)
            kt = (k_all[:, hs] * jnp.exp2(jnp.minimum(ref - a, HGRN_SPAN_LIMIT))).astype(BF16)
            band = lax.dot_general(qt, jnp.concatenate([kt, zpad_bf], axis=0), NT_DIMS,
                                   preferred_element_type=F32)
            band_ref[slot, h] = jnp.where(in_band, band, 0.0)

        @pl.when(steep)
        def _():
            for h in range(nh):
                hs = slice(h * LANES, (h + 1) * LANES)
                q, a = aq_ref[0, rows, hs], a_all[:, hs]
                sc = jnp.zeros((chunk, LANES), F32)
                for j in range(sub):
                    kj = jnp.concatenate(
                        [jnp.broadcast_to(ks_ref[slot, i * sub + j:i * sub + j + 1, hs], (sub, LANES))
                         for i in range(nsub)], axis=0)
                    aj = jnp.concatenate(
                        [jnp.broadcast_to(as_ref[slot, i * sub + j:i * sub + j + 1, hs], (sub, LANES))
                         for i in range(nsub)], axis=0)
                    w = q * kj * jnp.exp2(a - aj)
                    col = jnp.where(rmod >= j, jnp.sum(w, axis=-1, keepdims=True), 0.0)
                    sc = jnp.where(lane_off == j, col, sc)
                band_ref[slot, h] = sc

    def chunk_body(c, carry):
        slot = c % 2
        r0 = pl.multiple_of(c * chunk, chunk)
        rows = pl.ds(r0, chunk)
        k_all = ks_ref[slot]
        a_all = as_ref[slot]
        ends = sub_ends(a_all)

        for h in range(nh):
            hs = slice(h * LANES, (h + 1) * LANES)
            q = aq_ref[0, rows, hs]
            v = ai_ref[0, rows, hs]
            g = ag_ref[0, rows, hs]
            k, a = k_all[:, hs], a_all[:, hs]
            st = st_ref[h]
            v_bf = v.astype(BF16)

            o = lax.dot_general((q * jnp.exp2(a)).astype(BF16), st.astype(BF16), NT_DIMS,
                                preferred_element_type=F32)

            blocks = [jnp.zeros((sub, LANES), F32)]
            for i in range(1, nsub):
                lo, hi = i * sub, (i + 1) * sub
                r = ends[i - 1][:, hs]
                qt = (q[lo:hi] * jnp.exp2(a[lo:hi] - r)).astype(BF16)
                kt = jnp.where(row < lo, k * jnp.exp2(r - a), 0.0).astype(BF16)
                blocks.append(lax.dot_general(qt, jnp.concatenate([kt, zpad_bf], axis=0), NT_DIMS,
                                              preferred_element_type=F32))
            sc = (jnp.concatenate(blocks, axis=0) if nsub > 1 else blocks[0]) + band_ref[slot, h]
            o = o + jnp.dot(sc.astype(BF16), jnp.concatenate([v_bf, zpad_bf], axis=0),
                            preferred_element_type=F32)

            a_last = a[chunk - 1:chunk, :]
            kt = (k * jnp.exp2(a_last - a)).astype(BF16)
            st_ref[h] = st * jnp.exp2(a_last) + lax.dot_general(
                v_bf, kt, TN_DIMS, preferred_element_type=F32)

            y = o * lax.rsqrt(jnp.mean(o * o, axis=-1, keepdims=True) + EPS) * gain_all[:, hs]
            y = y * (g * jax.nn.sigmoid(g))
            o_ref[0, rows, hs] = y.astype(o_ref.dtype)
        if nchunks > 1:
            prepare(jnp.minimum(c + 1, nchunks - 1), 1 - slot)
        return carry

    prepare(0, 0)
    lax.fori_loop(0, nchunks, chunk_body, 0)

    @pl.when(t == pl.num_programs(1) - 1)
    def _():
        for h in range(nh):
            sout_ref[0, h] = st_ref[h].T


def _hgrn(hg, lb_param, gain, state0, layer, *, tb, chunk, sub, valid_len, out_dtype):
    bn, t_len, d4 = hg.shape
    dh = d4 // 4
    nh = dh // LANES
    depth = lb_param.shape[0]
    blk = lambda c: pl.BlockSpec((1, tb, dh), lambda b, t: (b, t, c))
    in_specs = [pl.BlockSpec((depth, dh), lambda b, t: (0, 0)),
                blk(0), blk(1), blk(2), blk(3),
                pl.BlockSpec((1, dh), lambda b, t: (0, 0))]
    args = [lb_param, hg, hg, hg, hg, gain.reshape(1, dh)]
    if state0 is not None:
        in_specs.append(pl.BlockSpec((1, 1, nh, LANES, LANES), lambda b, t: (layer, b, 0, 0, 0)))
        args.append(state0)
    return pl.pallas_call(
        functools.partial(_hgrn_kernel, layer=layer, chunk=chunk, sub=sub,
                          valid_len=valid_len, zero_init=state0 is None),
        grid=(bn, t_len // tb),
        in_specs=in_specs,
        out_specs=[pl.BlockSpec((1, tb, dh), lambda b, t: (b, t, 0)),
                   pl.BlockSpec((1, nh, LANES, LANES), lambda b, t: (b, 0, 0, 0))],
        out_shape=[jax.ShapeDtypeStruct((bn, t_len, dh), out_dtype),
                   jax.ShapeDtypeStruct((bn, nh, LANES, LANES), F32)],
        scratch_shapes=[pltpu.VMEM((nh, LANES, LANES), F32),
                        pltpu.VMEM((2, chunk, dh), F32),
                        pltpu.VMEM((2, chunk, dh), F32),
                        pltpu.VMEM((2, nh, chunk, LANES), F32)],
        compiler_params=_params("parallel", "arbitrary"),
        name="hgrn",
    )(*args)


def _fox_prompt_kernel(q_ref, kt_ref, v_ref, cum_ref, cumt_ref, gain_ref, o_ref,
                       s_ref, p_ref, fill_ref, m_ref, l_ref, acc_ref, *, scale):
    tq = q_ref.shape[1]
    rc = FOX_ROW_CHUNK
    band = max(tq // FOX_DIAG_BANDS, LANES)
    assert tq % band == 0 and band % rc == 0 and band % LANES == 0
    dh = LANES // 2
    hp = pl.program_id(1)
    qi = pl.program_id(2)
    lane = lax.broadcasted_iota(jnp.int32, (1, LANES), 1)
    q = q_ref[0] * (scale * LOG2E)
    cum = cum_ref[0]
    hl = lax.broadcasted_iota(jnp.int32, (1, H_FOX), 1)

    heads = range(2)
    qms = []
    for hh in heads:
        qms.append(jnp.where((lane // dh) == hh, q, 0.0).astype(BF16))
        cq = jnp.sum(jnp.where(hl == 2 * hp + hh, cum, 0.0), axis=-1, keepdims=True)
        fill_ref[hh] = jnp.broadcast_to((NEG - cq) * LOG2E, (tq, LANES))
        m_ref[hh] = jnp.full((tq, LANES), -jnp.inf, F32)
        l_ref[hh] = jnp.zeros((tq, LANES), F32)
        acc_ref[hh] = jnp.zeros((tq, LANES), F32)

    def key_bias(hh, kb):
        k0 = pl.multiple_of(kb * tq, tq)
        return cumt_ref[0, pl.ds(2 * hp + hh, 1), pl.ds(k0, tq)] * LOG2E

    def scores(kb, masked):
        k0 = pl.multiple_of(kb * tq, tq)
        kt = kt_ref[0, :, pl.ds(k0, tq)].astype(BF16)
        for hh in heads:
            if masked:
                for r1 in range(band, tq + 1, band):
                    s_ref[hh, kb, r1 - band:r1, :r1] = jnp.dot(
                        qms[hh][r1 - band:r1], kt[:, :r1], preferred_element_type=F32)
            else:
                s_ref[hh, kb] = jnp.dot(qms[hh], kt, preferred_element_type=F32)
        for hh in heads:
            ck = key_bias(hh, kb)
            for r in range(tq // rc):
                rs = slice(r * rc, (r + 1) * rc)
                m = m_ref[hh, rs, :]
                for j in range(tq // LANES):
                    cs = slice(j * LANES, (j + 1) * LANES)
                    if masked and j * LANES > (r + 1) * rc - 1:
                        continue
                    s = s_ref[hh, kb, rs, cs] - ck[:, cs]
                    if masked and (j + 1) * LANES - 1 > r * rc:
                        rid = r * rc + lax.broadcasted_iota(jnp.int32, (rc, LANES), 0)
                        cid = j * LANES + lax.broadcasted_iota(jnp.int32, (rc, LANES), 1)
                        s = jnp.where(rid >= cid, s, fill_ref[hh, rs, :])
                    if masked:
                        s_ref[hh, kb, rs, cs] = s
                    m = jnp.maximum(m, s)
                m_ref[hh, rs, :] = m

    def weights(kb, masked):
        k0 = pl.multiple_of(kb * tq, tq)
        v = v_ref[0, pl.ds(k0, tq), :].astype(BF16)
        for hh in heads:
            ck = key_bias(hh, kb)
            for r in range(tq // rc):
                rs = slice(r * rc, (r + 1) * rc)
                m = m_ref[hh, rs, :]
                lsum = l_ref[hh, rs, :]
                for j in range(tq // LANES):
                    cs = slice(j * LANES, (j + 1) * LANES)
                    if masked and j * LANES > (r + 1) * rc - 1:
                        if j * LANES < -(-(r + 1) * rc // band) * band:
                            p_ref[hh, rs, cs] = jnp.zeros((rc, LANES), BF16)
                        continue
                    s = s_ref[hh, kb, rs, cs]
                    if not masked:
                        s = s - ck[:, cs]
                    p = jnp.exp2(s - m)
                    lsum = lsum + p
                    p_ref[hh, rs, cs] = p.astype(BF16)
                l_ref[hh, rs, :] = lsum
            if masked:
                for r1 in range(band, tq + 1, band):
                    acc_ref[hh, r1 - band:r1, :] += jnp.dot(p_ref[hh, r1 - band:r1, :r1], v[:r1],
                                                            preferred_element_type=F32)
            else:
                acc_ref[hh] += jnp.dot(p_ref[hh], v, preferred_element_type=F32)

    def loop(fn):
        def body(kb, carry):
            fn(kb, masked=False)
            return carry
        lax.fori_loop(0, qi, body, 0)
        fn(qi, masked=True)

    loop(scores)
    for hh in heads:
        m_ref[hh] = jnp.broadcast_to(jnp.max(m_ref[hh], axis=-1, keepdims=True), (tq, LANES))
    loop(weights)

    low = (lane // dh) == 0
    o = jnp.where(low, acc_ref[0] / jnp.sum(l_ref[0], axis=-1, keepdims=True),
                  acc_ref[1] / jnp.sum(l_ref[1], axis=-1, keepdims=True))
    sq = o * o
    ss0 = jnp.sum(jnp.where(low, sq, 0.0), axis=-1, keepdims=True)
    ss1 = jnp.sum(jnp.where(low, 0.0, sq), axis=-1, keepdims=True)
    inv = jnp.where(low, lax.rsqrt(ss0 / dh + EPS), lax.rsqrt(ss1 / dh + EPS))
    o_ref[0] = (o * inv * gain_ref[...]).astype(o_ref.dtype)


def _fox_prompt(q, kt_stack, layer, v, cum, cumt, gain, tq=1024):
    bn, t_len, d = q.shape
    tq = min(tq, t_len)
    nhp = d // LANES
    scale = float((LANES // 2) ** -0.5)
    return pl.pallas_call(
        functools.partial(_fox_prompt_kernel, scale=scale),
        grid=(bn, nhp, t_len // tq),
        in_specs=[
            pl.BlockSpec((1, tq, LANES), lambda b, h, i: (b, i, h)),
            pl.BlockSpec((None, 1, LANES, t_len), lambda b, h, i: (layer, b, h, 0)),
            pl.BlockSpec((1, t_len, LANES), lambda b, h, i: (b, 0, h)),
            pl.BlockSpec((1, tq, H_FOX), lambda b, h, i: (b, i, 0)),
            pl.BlockSpec((1, H_FOX, t_len), lambda b, h, i: (b, 0, 0)),
            pl.BlockSpec((1, LANES), lambda b, h, i: (0, h)),
        ],
        out_specs=pl.BlockSpec((1, tq, LANES), lambda b, h, i: (b, i, h)),
        out_shape=jax.ShapeDtypeStruct((bn, t_len, d), BF16),
        scratch_shapes=[pltpu.VMEM((2, t_len // tq, tq, tq), F32),
                        pltpu.VMEM((2, tq, tq), BF16),
                        pltpu.VMEM((2, tq, LANES), F32),
                        pltpu.VMEM((2, tq, LANES), F32),
                        pltpu.VMEM((2, tq, LANES), F32),
                        pltpu.VMEM((2, tq, LANES), F32)],
        compiler_params=_params("parallel", "parallel", "arbitrary"),
        name="fox_prompt",
    )(q, kt_stack, v, cum, cumt, gain.reshape(1, d))


def _suffix_kernel(lf_ref, tab_ref):
    tp, nh, n = lf_ref.shape
    i = lax.broadcasted_iota(jnp.int32, (n, n), 0)
    j = lax.broadcasted_iota(jnp.int32, (n, n), 1)
    lf = lf_ref[...].reshape(tp * nh, n)
    suf = _dot01_right(lf, (i > j).astype(BF16))
    tot = _dot01_right(lf, jnp.ones((n, n), BF16))
    tab_ref[:, :nh, :] = suf.reshape(tp, nh, n)
    tab_ref[:, nh:, :] = tot.reshape(tp, nh, n)


def _suffix(lf_t, tp=512):
    pages, nh, n = lf_t.shape
    tp = min(tp, pages)
    assert pages % tp == 0
    return pl.pallas_call(
        _suffix_kernel,
        grid=(pages // tp,),
        in_specs=[pl.BlockSpec((tp, nh, n), lambda i: (i, 0, 0))],
        out_specs=pl.BlockSpec((tp, 2 * nh, n), lambda i: (i, 0, 0)),
        out_shape=jax.ShapeDtypeStruct((pages, 2 * nh, n), F32),
        compiler_params=_params("parallel"),
        name="suffix",
    )(lf_t)


def _fox_decode_kernel(pt_ref, q_ref, kn_ref, vn_ref, cq_ref, gain_ref, tab_ref, ck_hbm, cv_hbm,
                       o_ref, m_ref, l_ref, acc_ref, carry_ref, qbd_ref, kbuf, vbuf, sem,
                       *, pps, scale, first_page):
    steps = pl.num_programs(1)
    n_pages = steps * pps
    b = pl.program_id(0)
    j = pl.program_id(1)
    t_new = q_ref.shape[1]
    d = q_ref.shape[2]
    dh = d // H_FOX
    rows = t_new * H_FOX
    page = kbuf.shape[3]

    g = b * steps + j
    total = pl.num_programs(0) * steps

    def page_copies(gg, slot):
        bb = gg // steps
        jj = gg - bb * steps
        copies = []
        for i in range(pps):
            pid = first_page + pt_ref[bb, n_pages - 1 - (jj * pps + i)]
            copies.append(pltpu.make_async_copy(ck_hbm.at[pid], kbuf.at[slot, i], sem.at[slot, 0, i]))
            copies.append(pltpu.make_async_copy(cv_hbm.at[pid], vbuf.at[slot, i], sem.at[slot, 1, i]))
        return copies

    @pl.when(g == 0)
    def _():
        for ahead in range(DECODE_SLOTS - 1):
            @pl.when(ahead < total)
            def _():
                for c in page_copies(ahead, ahead):
                    c.start()

    @pl.when(g + DECODE_SLOTS - 1 < total)
    def _():
        nxt = g + DECODE_SLOTS - 1
        for c in page_copies(nxt, nxt % DECODE_SLOTS):
            c.start()

    slot = g % DECODE_SLOTS
    for c in page_copies(g, slot):
        c.wait()
    k_refs = [kbuf.at[slot, i] for i in range(pps)]
    v_refs = [vbuf.at[slot, i] for i in range(pps)]
    head_of_lane = lax.broadcasted_iota(jnp.int32, (H_FOX, d), 1) // dh
    head_of_row = lax.broadcasted_iota(jnp.int32, (H_FOX, d), 0)
    diag = head_of_lane == head_of_row
    cq = cq_ref[0]

    @pl.when(j == 0)
    def _():
        qbd = jnp.concatenate(
            [jnp.where(diag, q_ref[0, t:t + 1, :] * scale, 0.0) for t in range(t_new)], axis=0)
        qbd_ref[...] = qbd
        qrow = lax.broadcasted_iota(jnp.int32, (rows, 1), 0) // H_FOX
        s_new = []
        for s in range(t_new):
            dot_s = jnp.sum(qbd * kn_ref[0, s:s + 1, :], axis=-1, keepdims=True)
            cs = jnp.concatenate([cq[s * H_FOX:(s + 1) * H_FOX]] * t_new, axis=0)
            s_new.append(jnp.where(qrow >= s, dot_s + cq - cs, NEG))
        m0 = s_new[0]
        for s in range(1, t_new):
            m0 = jnp.maximum(m0, s_new[s])
        l0 = jnp.zeros((rows, 1), F32)
        acc0 = jnp.zeros((rows, d), F32)
        for s in range(t_new):
            p_s = jnp.exp(s_new[s] - m0)
            l0 = l0 + p_s
            acc0 = acc0 + p_s * vn_ref[0, s:s + 1, :]
        m_ref[...] = jnp.broadcast_to(m0, (rows, LANES))
        l_ref[...] = jnp.broadcast_to(l0, (rows, LANES))
        acc_ref[...] = acc0
        carry_ref[...] = jnp.zeros_like(carry_ref)

    qbd = qbd_ref[...].astype(BF16)
    carry = carry_ref[...]
    groups = [range(g0, min(g0 + DECODE_GROUP, pps)) for g0 in range(0, pps, DECODE_GROUP)]
    scores = []
    for group in groups:
        s_parts = []
        for i in group:
            kpg = k_refs[i][...].astype(BF16)
            s = jnp.dot(qbd, kpg, preferred_element_type=F32)
            pid = pt_ref[b, n_pages - 1 - (j * pps + i)]
            suf = tab_ref[pid, :H_FOX, :] + carry
            carry = carry + tab_ref[pid, H_FOX:, :]
            s_parts.append(s + jnp.concatenate([suf] * t_new, axis=0) + cq)
        scores.append(jnp.concatenate(s_parts, axis=1))
    carry_ref[...] = carry
    softmaxes = []
    for s_g in scores:
        m_g = jnp.max(s_g, axis=-1, keepdims=True)
        p = jnp.exp(s_g - m_g)
        softmaxes.append((m_g, jnp.sum(p, axis=-1, keepdims=True), p.astype(BF16)))
    partials = []
    for group, (m_g, l_g, p_bf) in zip(groups, softmaxes):
        vt = jnp.concatenate([v_refs[i][...] for i in group], axis=1).astype(BF16)
        partials.append((m_g, l_g, lax.dot_general(p_bf, vt, NT_DIMS, preferred_element_type=F32)))
    m_prev = m_ref[...][:, :1]
    m_new = m_prev
    for m_g, _, _ in partials:
        m_new = jnp.maximum(m_new, m_g)
    alpha = jnp.exp(m_prev - m_new)
    l_new = alpha * l_ref[...][:, :1]
    acc = alpha * acc_ref[...]
    for m_g, l_g, acc_g in partials:
        w_g = jnp.exp(m_g - m_new)
        l_new = l_new + w_g * l_g
        acc = acc + w_g * acc_g
    m_ref[...] = jnp.broadcast_to(m_new, m_ref.shape)
    l_ref[...] = jnp.broadcast_to(l_new, l_ref.shape)
    acc_ref[...] = acc

    @pl.when(j == pl.num_programs(1) - 1)
    def _():
        o = acc_ref[...] / l_ref[...][:, :1]
        o = jnp.where(jnp.concatenate([diag] * t_new, axis=0), o, 0.0)
        y = o * lax.rsqrt(jnp.sum(o * o, axis=-1, keepdims=True) / dh + EPS)
        out = jnp.concatenate(
            [jnp.sum(y[t * H_FOX:(t + 1) * H_FOX], axis=0, keepdims=True) for t in range(t_new)],
            axis=0)
        o_ref[0] = out * gain_ref[...]


def _fox_decode(q, k_new, v_new, cq_col, gain, cache_k, cache_v, suffix_tab, page_table,
                layer, n_pool, pps=8):
    bn, t_new, d = q.shape
    page = cache_k.shape[2]
    n_pages = page_table.shape[1]
    pps = min(pps, n_pages)
    assert n_pages % pps == 0
    rows = t_new * H_FOX
    scale = float((d // H_FOX) ** -0.5)
    tok = pl.BlockSpec((1, t_new, d), lambda b, j, pt: (b, 0, 0))
    in_specs = [tok, tok, tok,
                pl.BlockSpec((1, rows, 1), lambda b, j, pt: (b, 0, 0)),
                pl.BlockSpec((1, d), lambda b, j, pt: (0, 0)),
                _resident((None, n_pool, 2 * H_FOX, page), lambda b, j, pt: (layer, 0, 0, 0)),
                pl.BlockSpec(memory_space=pl.ANY),
                pl.BlockSpec(memory_space=pl.ANY)]
    grid_spec = pltpu.PrefetchScalarGridSpec(
        num_scalar_prefetch=1,
        grid=(bn, n_pages // pps),
        in_specs=in_specs,
        out_specs=pl.BlockSpec((1, t_new, d), lambda b, j, pt: (b, 0, 0)),
        scratch_shapes=[pltpu.VMEM((rows, LANES), F32),
                        pltpu.VMEM((rows, LANES), F32),
                        pltpu.VMEM((rows, d), F32),
                        pltpu.VMEM((H_FOX, page), F32),
                        pltpu.VMEM((rows, d), F32),
                        pltpu.VMEM((DECODE_SLOTS, pps, d, page), F32),
                        pltpu.VMEM((DECODE_SLOTS, pps, d, page), F32),
                        pltpu.SemaphoreType.DMA((DECODE_SLOTS, 2, pps))],
    )
    return pl.pallas_call(
        functools.partial(_fox_decode_kernel, pps=pps, scale=scale, first_page=layer * n_pool),
        grid_spec=grid_spec,
        out_shape=jax.ShapeDtypeStruct((bn, t_new, d), F32),
        compiler_params=_params("arbitrary", "arbitrary"),
        name="fox_decode",
    )(page_table, q, k_new, v_new, cq_col, gain.reshape(1, d), suffix_tab, cache_k, cache_v)


def kernel(x_prompt, x_sample, cache_k, cache_v, cache_logf, state_hgrn, page_table, norm_ffn1, ffn1_w_in, ffn1_w_out, norm_mix, w_in_mix, hgrn_lb, fox_f_bias, hgrn_out_gain, fox_out_gain, w_out_mix, norm_ffn2, ffn2_w_in, ffn2_w_out, norm_final):
    depth = norm_ffn1.shape[0]
    bn, t_len, d = x_prompt.shape
    db, t_new, _ = x_sample.shape
    d_fox = fox_out_gain.shape[1]
    d_h = hgrn_out_gain.shape[1]
    dh_fox = d_fox // H_FOX
    n_pool, page = cache_k.shape[1], cache_k.shape[2]
    d_main = 4 * d_h + 3 * d_fox

    w1_in, w1_out = ffn1_w_in.astype(BF16), ffn1_w_out.astype(BF16)
    w2_in, w2_out = ffn2_w_in.astype(BF16), ffn2_w_out.astype(BF16)
    w_mix = w_in_mix.astype(BF16)
    w_f = jnp.pad(w_in_mix[:, :, d_main:], ((0, 0), (0, 0), (0, LANES - H_FOX))).astype(BF16)
    f_bias = jnp.pad(fox_f_bias, ((0, 0), (0, LANES - H_FOX))).reshape(depth, 1, LANES)
    w_out = w_out_mix.astype(BF16)

    ck = jnp.transpose(cache_k, (0, 1, 3, 4, 2)).reshape(depth * n_pool, d_fox, page)
    cv = jnp.transpose(cache_v, (0, 1, 3, 4, 2)).reshape(depth * n_pool, d_fox, page)
    lf_t = jnp.swapaxes(cache_logf, 2, 3).reshape(depth * n_pool, H_FOX, page)
    suffix_tab = _suffix(lf_t).reshape(depth, n_pool, 2 * H_FOX, page)

    bf16_rows = 2 * SUBLANES
    t_pad = -(-t_new // bf16_rows) * bf16_rows

    def run(x, seq_len, nseq, prompt):
        ks, vs, lfs, ss = [], [], [], []
        kv_stack = None
        for l in range(depth):
            x = _ffn(x, norm_ffn1[l], w1_in, w1_out, l)
            if prompt:
                hg, q, v, lf, cum, cumt, *kv_stack = _inproj(
                    x, norm_mix[l], w_mix, w_f, f_bias, l, seq_len, 4 * d_h, d_fox,
                    earlier=kv_stack, tm=min(512, seq_len))
                o_a, s_fin = _hgrn(hg.reshape(nseq, seq_len, 4 * d_h), hgrn_lb, hgrn_out_gain[l], None, l,
                                   tb=min(2048, seq_len), chunk=64, sub=16, valid_len=None, out_dtype=BF16)
                o_a = o_a.reshape(nseq * seq_len, d_h)
                tiles = cumt.shape[0] // nseq
                cumt_b = cumt.reshape(nseq, tiles, H_FOX, -1).transpose(0, 2, 1, 3).reshape(nseq, H_FOX, seq_len)
                o_f = _fox_prompt(q.reshape(nseq, seq_len, d_fox), kv_stack[0], l,
                                  v.reshape(nseq, seq_len, d_fox), cum.reshape(nseq, seq_len, H_FOX),
                                  cumt_b, fox_out_gain[l]).reshape(nseq * seq_len, d_fox)
            else:
                hg, q, k, v, lf, cum, cumt = _inproj(x, norm_mix[l], w_mix, w_f, f_bias, l, seq_len,
                                                     4 * d_h, d_fox)
                ks.append(k.reshape(nseq, seq_len, H_FOX, dh_fox))
                vs.append(v.reshape(nseq, seq_len, H_FOX, dh_fox))
                hg_p = jnp.pad(hg.reshape(nseq, seq_len, 4 * d_h), ((0, 0), (0, t_pad - seq_len), (0, 0)))
                o_a, s_fin = _hgrn(hg_p, hgrn_lb, hgrn_out_gain[l], state_hgrn, l,
                                   tb=t_pad, chunk=t_pad, sub=t_pad, valid_len=seq_len, out_dtype=F32)
                o_a = o_a[:, :seq_len].reshape(nseq * seq_len, d_h).astype(BF16)
                o_f = _fox_decode(q.reshape(nseq, seq_len, d_fox), k.reshape(nseq, seq_len, d_fox),
                                  v.reshape(nseq, seq_len, d_fox), cum.reshape(nseq, seq_len * H_FOX, 1),
                                  fox_out_gain[l], ck, cv, suffix_tab, page_table, l, n_pool)
                o_f = o_f.reshape(nseq * seq_len, d_fox).astype(BF16)
            x = _ffn(x, norm_ffn2[l], w2_in, w2_out, l, mixer=(o_a, o_f, w_out),
                     final_g=norm_final if l == depth - 1 else None)
            lfs.append(lf.reshape(nseq, seq_len, H_FOX))
            ss.append(s_fin)
        if prompt:
            k_all, v_all = (a.reshape(depth, nseq, H_FOX, dh_fox, seq_len).transpose(0, 1, 4, 2, 3)
                            for a in kv_stack)
        else:
            k_all, v_all = jnp.stack(ks), jnp.stack(vs)
        return (x.reshape(nseq, seq_len, d), k_all, v_all, jnp.stack(lfs), jnp.stack(ss))

    y_p, k_p, v_p, lf_p, s_p = run(x_prompt.reshape(bn * t_len, d), t_len, bn, True)
    y_s, k_s, v_s, lf_s, s_s = run(x_sample.reshape(db * t_new, d), t_new, db, False)
    return (y_p, y_s, k_p, v_p, lf_p, s_p, k_s, v_s, lf_s, s_s)
```

```python
import functools

import jax
import jax.numpy as jnp
from jax import lax
from jax.experimental import pallas as pl
from jax.experimental.pallas import tpu as pltpu

F32 = jnp.float32
BF16 = jnp.bfloat16

EPS = 1e-6
TINY = 1e-30
NEG = -1e30

LOG2E = 1.4426950408889634
H_HGRN = 4
H_FOX = 8
FOX_ROW_CHUNK = 64
FOX_DIAG_BANDS = 4
CUM_BLOCK = 256
HGRN_SPAN_LIMIT = 100.0
DECODE_GROUP = 2
DECODE_SLOTS = 3
LANES = 128
SUBLANES = 8
VMEM_LIMIT = 56 * 1024 * 1024

NT_DIMS = (((1,), (1,)), ((), ()))
TN_DIMS = (((0,), (0,)), ((), ()))


def _params(*sem):
    return pltpu.CompilerParams(dimension_semantics=sem, vmem_limit_bytes=VMEM_LIMIT)


def _resident(shape, index_map):
    return pl.BlockSpec(shape, index_map, pipeline_mode=pl.Buffered(1))


def _rms(x, g):
    return x * lax.rsqrt(jnp.mean(x * x, axis=-1, keepdims=True) + EPS) * g


def _split3(x):
    hi = x.astype(BF16)
    r = x - hi.astype(F32)
    mid = r.astype(BF16)
    lo = (r - mid.astype(F32)).astype(BF16)
    return hi, mid, lo


def _dot01_left(m01, x):
    return sum(jnp.dot(m01, t, preferred_element_type=F32) for t in _split3(x))


def _dot01_right(x, m01):
    return sum(jnp.dot(t, m01, preferred_element_type=F32) for t in _split3(x))


def _ffn_kernel(x_ref, g_ref, wa_ref, wb_ref, wo_ref, *rest, final_norm, mixer_out):
    rest = list(rest)
    o_ref = rest.pop()
    x = x_ref[...]
    if mixer_out:
        oa_ref, of_ref, wma_ref, wmf_ref = rest[:4]
        del rest[:4]
        x = (x + jnp.dot(oa_ref[...], wma_ref[...], preferred_element_type=F32)
             + jnp.dot(of_ref[...], wmf_ref[...], preferred_element_type=F32))
    if final_norm:
        (gf_ref,) = rest
    xn = _rms(x, g_ref[...]).astype(BF16)
    a = jnp.dot(xn, wa_ref[...], preferred_element_type=F32)
    b = jnp.dot(xn, wb_ref[...], preferred_element_type=F32)
    h = (a * jax.nn.sigmoid(a) * b).astype(BF16)
    y = x + 0.5 * jnp.dot(h, wo_ref[...], preferred_element_type=F32)
    if final_norm:
        y = _rms(y, gf_ref[...])
    o_ref[...] = y


def _ffn(x, g, w_in, w_out, layer, mixer=None, final_g=None, tm=512):
    n, d = x.shape
    ff = w_out.shape[1]
    tm = min(tm, n)
    in_specs = [
        pl.BlockSpec((tm, d), lambda i: (i, 0)),
        _resident((1, d), lambda i: (0, 0)),
        _resident((None, d, ff), lambda i: (layer, 0, 0)),
        _resident((None, d, ff), lambda i: (layer, 0, 1)),
        _resident((None, ff, d), lambda i: (layer, 0, 0)),
    ]
    args = [x, g.reshape(1, d), w_in, w_in, w_out]
    if mixer is not None:
        o_a, o_f, w_mix_out = mixer
        dm = o_a.shape[1]
        assert o_f.shape[1] == dm and w_mix_out.shape[1] == 2 * dm
        in_specs += [pl.BlockSpec((tm, dm), lambda i: (i, 0)),
                     pl.BlockSpec((tm, dm), lambda i: (i, 0)),
                     _resident((None, dm, d), lambda i: (layer, 0, 0)),
                     _resident((None, dm, d), lambda i: (layer, 1, 0))]
        args += [o_a, o_f, w_mix_out, w_mix_out]
    if final_g is not None:
        in_specs.append(_resident((1, d), lambda i: (0, 0)))
        args.append(final_g.reshape(1, d))
    return pl.pallas_call(
        functools.partial(_ffn_kernel, final_norm=final_g is not None, mixer_out=mixer is not None),
        grid=(n // tm,),
        in_specs=in_specs,
        out_specs=pl.BlockSpec((tm, d), lambda i: (i, 0)),
        out_shape=jax.ShapeDtypeStruct((n, d), F32),
        compiler_params=_params("parallel"),
        name="ffn",
    )(*args)


def _inproj_kernel(x_ref, g_ref, w_ref, wf_ref, fb_ref, *rest,
                   seq_len, d_hg, d_fox, feature_major, n_earlier):
    rest = list(rest)
    carry_ref = rest.pop()
    if n_earlier:
        pk_ref, pv_ref = rest[:2]
        del rest[:2]
    if feature_major:
        hg_ref, q_ref, v_ref, lf_ref, cum_ref, cumt_ref, kt_ref, vt_ref = rest
    else:
        hg_ref, q_ref, k_ref, v_ref, lf_ref, cum_ref, cumt_ref = rest
    tm = x_ref.shape[0]
    i = pl.program_id(0)
    xn = _rms(x_ref[...], g_ref[...]).astype(BF16)
    p = jnp.dot(xn, w_ref[...], preferred_element_type=F32)
    hg_ref[...] = p[:, :d_hg]
    q_ref[...] = p[:, d_hg:d_hg + d_fox]
    k = p[:, d_hg + d_fox:d_hg + 2 * d_fox]
    v = p[:, d_hg + 2 * d_fox:d_hg + 3 * d_fox]
    v_ref[...] = v
    if feature_major:
        if n_earlier:
            kt_ref[:n_earlier, 0] = pk_ref[:, 0]
            vt_ref[:n_earlier, 0] = pv_ref[:, 0]
        kt_ref[n_earlier, 0] = k.T
        vt_ref[n_earlier, 0] = v.T
    else:
        k_ref[...] = k

    z = jnp.dot(xn, wf_ref[...], preferred_element_type=F32) + fb_ref[...]
    lf = jnp.minimum(z, 0.0) - jnp.log1p(jnp.exp(-jnp.abs(z)))

    bs = min(CUM_BLOCK, tm)
    assert seq_len >= tm or bs % seq_len == 0
    row = lax.broadcasted_iota(jnp.int32, (bs, bs), 0)
    col = lax.broadcasted_iota(jnp.int32, (bs, bs), 1)
    same = col <= row
    if seq_len < tm:
        same = same & ((row // seq_len) == (col // seq_len))
    same = same.astype(BF16)
    offset = jnp.zeros((1, LANES), F32)
    if seq_len > tm:
        tiles_per_seq = seq_len // tm

        @pl.when(i % tiles_per_seq == 0)
        def _():
            carry_ref[...] = jnp.zeros_like(carry_ref)

        offset = carry_ref[...]
    blocks = []
    for r0 in range(0, tm, bs):
        blocks.append(_dot01_left(same, lf[r0:r0 + bs]) + offset)
        if seq_len >= tm:
            offset = blocks[-1][bs - 1:bs, :]
    cum = jnp.concatenate(blocks, axis=0) if len(blocks) > 1 else blocks[0]
    if seq_len > tm:
        carry_ref[...] = offset
    lf_ref[...] = lf[:, :H_FOX]
    cum_ref[...] = cum[:, :H_FOX]
    cumt_ref[0] = cum.T[:H_FOX, :]


def _inproj(x, g, w_all, w_f, f_bias, layer, seq_len, d_hg, d_fox, earlier=None, tm=512):
    n, d = x.shape
    tm = min(tm, n)
    assert seq_len % tm == 0 or tm % seq_len == 0
    d_main = d_hg + 3 * d_fox
    nt = n // tm
    tok = lambda w: pl.BlockSpec((tm, w), lambda i: (i, 0))
    tok_shape = lambda w: jax.ShapeDtypeStruct((n, w), F32)
    feature_major = seq_len % tm == 0
    n_earlier = 0 if earlier is None else earlier[0].shape[0]
    in_specs = [
        tok(d),
        _resident((1, d), lambda i: (0, 0)),
        _resident((None, d, d_main), lambda i: (layer, 0, 0)),
        _resident((None, d, LANES), lambda i: (layer, 0, 0)),
        _resident((None, 1, LANES), lambda i: (layer, 0, 0)),
    ]
    args = [x, g.reshape(1, d), w_all, w_f, f_bias]
    tail_specs = [tok(H_FOX), tok(H_FOX), pl.BlockSpec((1, H_FOX, tm), lambda i: (i, 0, 0))]
    tail_shape = [tok_shape(H_FOX), tok_shape(H_FOX), jax.ShapeDtypeStruct((nt, H_FOX, tm), F32)]
    if feature_major:
        tps = seq_len // tm
        stack = lambda layers: pl.BlockSpec((layers, 1, d_fox, tm), lambda i: (0, i // tps, 0, i % tps))
        if n_earlier:
            in_specs += [stack(n_earlier)] * 2
            args += list(earlier)
        out_specs = [tok(d_hg), tok(d_fox), tok(d_fox)] + tail_specs + [stack(n_earlier + 1)] * 2
        out_shape = ([tok_shape(d_hg), tok_shape(d_fox), tok_shape(d_fox)] + tail_shape
                     + [jax.ShapeDtypeStruct((n_earlier + 1, n // seq_len, d_fox, seq_len), F32)] * 2)
    else:
        assert earlier is None
        out_specs = [tok(d_hg), tok(d_fox), tok(d_fox), tok(d_fox)] + tail_specs
        out_shape = [tok_shape(d_hg), tok_shape(d_fox), tok_shape(d_fox), tok_shape(d_fox)] + tail_shape
    return pl.pallas_call(
        functools.partial(_inproj_kernel, seq_len=seq_len, d_hg=d_hg, d_fox=d_fox,
                          feature_major=feature_major, n_earlier=n_earlier),
        grid=(nt,),
        in_specs=in_specs,
        out_specs=out_specs,
        out_shape=out_shape,
        scratch_shapes=[pltpu.VMEM((1, LANES), F32)],
        compiler_params=_params("arbitrary"),
        name="inproj",
    )(*args)


def _hgrn_kernel(lbp_ref, aq_ref, af_ref, ai_ref, ag_ref, gain_ref, *rest,
                 layer, chunk, sub, valid_len, zero_init):
    if zero_init:
        o_ref, sout_ref, st_ref, ks_ref, as_ref, band_ref = rest
    else:
        s0_ref, o_ref, sout_ref, st_ref, ks_ref, as_ref, band_ref = rest
    tb = aq_ref.shape[1]
    nh = aq_ref.shape[2] // LANES
    nchunks = tb // chunk
    t = pl.program_id(1)

    @pl.when(t == 0)
    def _():
        for h in range(nh):
            if zero_init:
                st_ref[h] = jnp.zeros((LANES, LANES), F32)
            else:
                st_ref[h] = s0_ref[0, 0, h].T

    lbp = lbp_ref[...]
    e = jnp.exp(lbp - jnp.max(lbp, axis=0, keepdims=True))
    prob = e / jnp.sum(e, axis=0, keepdims=True)
    lb = jnp.zeros((1, nh * LANES), F32)
    for j in range(1, layer + 1):
        lb = lb + prob[j:j + 1, :]

    nsub = chunk // sub
    row = lax.broadcasted_iota(jnp.int32, (chunk, 1), 0)
    rmod = row % sub
    lane_off = lax.broadcasted_iota(jnp.int32, (chunk, LANES), 1) - (row - rmod)
    in_band = (lane_off >= 0) & (lane_off <= rmod)
    tri =(lax.broadcasted_iota(jnp.int32, (chunk, chunk), 1)
           <= lax.broadcasted_iota(jnp.int32, (chunk, chunk), 0)).astype(BF16)
    zpad_bf = jnp.zeros((LANES - chunk, LANES), BF16)
    gain_all = gain_ref[...]

    def sub_ends(a_all):
        return [a_all[(i + 1) * sub - 1:(i + 1) * sub, :] for i in range(nsub)]

    def prepare(c, slot):
        r0 = pl.multiple_of(c * chunk, chunk)
        rows = pl.ds(r0, chunk)
        f_all = lb + (1.0 - lb) * jax.nn.sigmoid(af_ref[0, rows, :])
        logf_all = jnp.log(jnp.maximum(f_all, TINY))
        k_all = 1.0 - f_all
        if valid_len is not None:
            live = (t * tb + r0 + row) < valid_len
            logf_all = jnp.where(live, logf_all, 0.0)
            k_all = jnp.where(live, k_all, 0.0)
        a_all = _dot01_left(tri, logf_all) * LOG2E
        ks_ref[slot] = k_all
        as_ref[slot] = a_all

        ends = sub_ends(a_all)
        span = -ends[0]
        for i in range(1, nsub):
            span = jnp.maximum(span, ends[i - 1] - ends[i])
        steep = jnp.max(span) > HGRN_SPAN_LIMIT

        ref_all = jnp.concatenate(
            [jnp.zeros((sub, nh * LANES), F32)]
            + [jnp.broadcast_to(ends[i - 1], (sub, nh * LANES)) for i in range(1, nsub)], axis=0)
        for h in range(nh):
            hs = slice(h * LANES, (h + 1) * LANES)
            a, ref = a_all[:, hs], ref_all[:, hs]
            qt = (aq_ref[0, rows, hs] * jnp.exp2(a - ref)).astype(BF16)
            kt = (k_all[:, hs] * jnp.exp2(jnp.minimum(ref - a, HGRN_SPAN_LIMIT))).astype(BF16)
            band = lax.dot_general(qt, jnp.concatenate([kt, zpad_bf], axis=0), NT_DIMS,
                                   preferred_element_type=F32)
            band_ref[slot, h] = jnp.where(in_band, band, 0.0)

        @pl.when(steep)
        def _():
            for h in range(nh):
                hs = slice(h * LANES, (h + 1) * LANES)
                q, a = aq_ref[0, rows, hs], a_all[:, hs]
                sc = jnp.zeros((chunk, LANES), F32)
                for j in range(sub):
                    kj = jnp.concatenate(
                        [jnp.broadcast_to(ks_ref[slot, i * sub + j:i * sub + j + 1, hs], (sub, LANES))
                         for i in range(nsub)], axis=0)
                    aj = jnp.concatenate(
                        [jnp.broadcast_to(as_ref[slot, i * sub + j:i * sub + j + 1, hs], (sub, LANES))
                         for i in range(nsub)], axis=0)
                    w = q * kj * jnp.exp2(a - aj)
                    col = jnp.where(rmod >= j, jnp.sum(w, axis=-1, keepdims=True), 0.0)
                    sc = jnp.where(lane_off == j, col, sc)
                band_ref[slot, h] = sc

    def chunk_body(c, carry):
        slot = c % 2
        r0 = pl.multiple_of(c * chunk, chunk)
        rows = pl.ds(r0, chunk)
        k_all = ks_ref[slot]
        a_all = as_ref[slot]
        ends = sub_ends(a_all)

        for h in range(nh):
            hs = slice(h * LANES, (h + 1) * LANES)
            q = aq_ref[0, rows, hs]
            v = ai_ref[0, rows, hs]
            g = ag_ref[0, rows, hs]
            k, a = k_all[:, hs], a_all[:, hs]
            st = st_ref[h]
            v_bf = v.astype(BF16)

            o = lax.dot_general((q * jnp.exp2(a)).astype(BF16), st.astype(BF16), NT_DIMS,
                                preferred_element_type=F32)

            blocks = [jnp.zeros((sub, LANES), F32)]
            for i in range(1, nsub):
                lo, hi = i * sub, (i + 1) * sub
                r = ends[i - 1][:, hs]
                qt = (q[lo:hi] * jnp.exp2(a[lo:hi] - r)).astype(BF16)
                kt = jnp.where(row < lo, k * jnp.exp2(r - a), 0.0).astype(BF16)
                blocks.append(lax.dot_general(qt, jnp.concatenate([kt, zpad_bf], axis=0), NT_DIMS,
                                              preferred_element_type=F32))
            sc = (jnp.concatenate(blocks, axis=0) if nsub > 1 else blocks[0]) + band_ref[slot, h]
            o = o + jnp.dot(sc.astype(BF16), jnp.concatenate([v_bf, zpad_bf], axis=0),
                            preferred_element_type=F32)

            a_last = a[chunk - 1:chunk, :]
            kt = (k * jnp.exp2(a_last - a)).astype(BF16)
            st_ref[h] = st * jnp.exp2(a_last) + lax.dot_general(
                v_bf, kt, TN_DIMS, preferred_element_type=F32)

            y = o * lax.rsqrt(jnp.mean(o * o, axis=-1, keepdims=True) + EPS) * gain_all[:, hs]
            y = y * (g * jax.nn.sigmoid(g))
            o_ref[0, rows, hs] = y.astype(o_ref.dtype)
        if nchunks > 1:
            prepare(jnp.minimum(c + 1, nchunks - 1), 1 - slot)
        return carry

    prepare(0, 0)
    lax.fori_loop(0, nchunks, chunk_body, 0)

    @pl.when(t == pl.num_programs(1) - 1)
    def _():
        for h in range(nh):
            sout_ref[0, h] = st_ref[h].T


def _hgrn(hg, lb_param, gain, state0, layer, *, tb, chunk, sub, valid_len, out_dtype):
    bn, t_len, d4 = hg.shape
    dh = d4 // 4
    nh = dh // LANES
    depth = lb_param.shape[0]
    blk = lambda c: pl.BlockSpec((1, tb, dh), lambda b, t: (b, t, c))
    in_specs = [pl.BlockSpec((depth, dh), lambda b, t: (0, 0)),
                blk(0), blk(1), blk(2), blk(3),
                pl.BlockSpec((1, dh), lambda b, t: (0, 0))]
    args = [lb_param, hg, hg, hg, hg, gain.reshape(1, dh)]
    if state0 is not None:
        in_specs.append(pl.BlockSpec((1, 1, nh, LANES, LANES), lambda b, t: (layer, b, 0, 0, 0)))
        args.append(state0)
    return pl.pallas_call(
        functools.partial(_hgrn_kernel, layer=layer, chunk=chunk, sub=sub,
                          valid_len=valid_len, zero_init=state0 is None),
        grid=(bn, t_len // tb),
        in_specs=in_specs,
        out_specs=[pl.BlockSpec((1, tb, dh), lambda b, t: (b, t, 0)),
                   pl.BlockSpec((1, nh, LANES, LANES), lambda b, t: (b, 0, 0, 0))],
        out_shape=[jax.ShapeDtypeStruct((bn, t_len, dh), out_dtype),
                   jax.ShapeDtypeStruct((bn, nh, LANES, LANES), F32)],
        scratch_shapes=[pltpu.VMEM((nh, LANES, LANES), F32),
                        pltpu.VMEM((2, chunk, dh), F32),
                        pltpu.VMEM((2, chunk, dh), F32),
                        pltpu.VMEM((2, nh, chunk, LANES), F32)],
        compiler_params=_params("parallel", "arbitrary"),
        name="hgrn",
    )(*args)


def _fox_prompt_kernel(q_ref, kt_ref, v_ref, cum_ref, cumt_ref, gain_ref, o_ref,
                       s_ref, p_ref, fill_ref, m_ref, l_ref, acc_ref, *, scale):
    tq = q_ref.shape[1]
    rc = FOX_ROW_CHUNK
    band = max(tq // FOX_DIAG_BANDS, LANES)
    assert tq % band == 0 and band % rc == 0 and band % LANES == 0
    dh = LANES // 2
    hp = pl.program_id(1)
    qi = pl.program_id(2)
    lane = lax.broadcasted_iota(jnp.int32, (1, LANES), 1)
    q = q_ref[0] * (scale * LOG2E)
    cum = cum_ref[0]
    hl = lax.broadcasted_iota(jnp.int32, (1, H_FOX), 1)

    heads = range(2)
    qms = []
    for hh in heads:
        qms.append(jnp.where((lane // dh) == hh, q, 0.0).astype(BF16))
        cq = jnp.sum(jnp.where(hl == 2 * hp + hh, cum, 0.0), axis=-1, keepdims=True)
        fill_ref[hh] = jnp.broadcast_to((NEG - cq) * LOG2E, (tq, LANES))
        m_ref[hh] = jnp.full((tq, LANES), -jnp.inf, F32)
        l_ref[hh] = jnp.zeros((tq, LANES), F32)
        acc_ref[hh] = jnp.zeros((tq, LANES), F32)

    def key_bias(hh, kb):
        k0 = pl.multiple_of(kb * tq, tq)
        return cumt_ref[0, pl.ds(2 * hp + hh, 1), pl.ds(k0, tq)] * LOG2E

    def scores(kb, masked):
        k0 = pl.multiple_of(kb * tq, tq)
        kt = kt_ref[0, :, pl.ds(k0, tq)].astype(BF16)
        for hh in heads:
            if masked:
                for r1 in range(band, tq + 1, band):
                    s_ref[hh, kb, r1 - band:r1, :r1] = jnp.dot(
                        qms[hh][r1 - band:r1], kt[:, :r1], preferred_element_type=F32)
            else:
                s_ref[hh, kb] = jnp.dot(qms[hh], kt, preferred_element_type=F32)
        for hh in heads:
            ck = key_bias(hh, kb)
            for r in range(tq // rc):
                rs = slice(r * rc, (r + 1) * rc)
                m = m_ref[hh, rs, :]
                for j in range(tq // LANES):
                    cs = slice(j * LANES, (j + 1) * LANES)
                    if masked and j * LANES > (r + 1) * rc - 1:
                        continue
                    s = s_ref[hh, kb, rs, cs] - ck[:, cs]
                    if masked and (j + 1) * LANES - 1 > r * rc:
                        rid = r * rc + lax.broadcasted_iota(jnp.int32, (rc, LANES), 0)
                        cid = j * LANES + lax.broadcasted_iota(jnp.int32, (rc, LANES), 1)
                        s = jnp.where(rid >= cid, s, fill_ref[hh, rs, :])
                    if masked:
                        s_ref[hh, kb, rs, cs] = s
                    m = jnp.maximum(m, s)
                m_ref[hh, rs, :] = m

    def weights(kb, masked):
        k0 = pl.multiple_of(kb * tq, tq)
        v = v_ref[0, pl.ds(k0, tq), :].astype(BF16)
        for hh in heads:
            ck = key_bias(hh, kb)
            for r in range(tq // rc):
                rs = slice(r * rc, (r + 1) * rc)
                m = m_ref[hh, rs, :]
                lsum = l_ref[hh, rs, :]
                for j in range(tq // LANES):
                    cs = slice(j * LANES, (j + 1) * LANES)
                    if masked and j * LANES > (r + 1) * rc - 1:
                        if j * LANES < -(-(r + 1) * rc // band) * band:
                            p_ref[hh, rs, cs] = jnp.zeros((rc, LANES), BF16)
                        continue
                    s = s_ref[hh, kb, rs, cs]
                    if not masked:
                        s = s - ck[:, cs]
                    p = jnp.exp2(s - m)
                    lsum = lsum + p
                    p_ref[hh, rs, cs] = p.astype(BF16)
                l_ref[hh, rs, :] = lsum
            if masked:
                for r1 in range(band, tq + 1, band):
                    acc_ref[hh, r1 - band:r1, :] += jnp.dot(p_ref[hh, r1 - band:r1, :r1], v[:r1],
                                                            preferred_element_type=F32)
            else:
                acc_ref[hh] += jnp.dot(p_ref[hh], v, preferred_element_type=F32)

    def loop(fn):
        def body(kb, carry):
            fn(kb, masked=False)
            return carry
        lax.fori_loop(0, qi, body, 0)
        fn(qi, masked=True)

    loop(scores)
    for hh in heads:
        m_ref[hh] = jnp.broadcast_to(jnp.max(m_ref[hh], axis=-1, keepdims=True), (tq, LANES))
    loop(weights)

    low = (lane // dh) == 0
    o = jnp.where(low, acc_ref[0] / jnp.sum(l_ref[0], axis=-1, keepdims=True),
                  acc_ref[1] / jnp.sum(l_ref[1], axis=-1, keepdims=True))
    sq = o * o
    ss0 = jnp.sum(jnp.where(low, sq, 0.0), axis=-1, keepdims=True)
    ss1 = jnp.sum(jnp.where(low, 0.0, sq), axis=-1, keepdims=True)
    inv = jnp.where(low, lax.rsqrt(ss0 / dh + EPS), lax.rsqrt(ss1 / dh + EPS))
    o_ref[0] = (o * inv * gain_ref[...]).astype(o_ref.dtype)


def _fox_prompt(q, kt_stack, layer, v, cum, cumt, gain, tq=1024):
    bn, t_len, d = q.shape
    tq = min(tq, t_len)
    nhp = d // LANES
    scale = float((LANES // 2) ** -0.5)
    return pl.pallas_call(
        functools.partial(_fox_prompt_kernel, scale=scale),
        grid=(bn, nhp, t_len // tq),
        in_specs=[
            pl.BlockSpec((1, tq, LANES), lambda b, h, i: (b, i, h)),
            pl.BlockSpec((None, 1, LANES, t_len), lambda b, h, i: (layer, b, h, 0)),
            pl.BlockSpec((1, t_len, LANES), lambda b, h, i: (b, 0, h)),
            pl.BlockSpec((1, tq, H_FOX), lambda b, h, i: (b, i, 0)),
            pl.BlockSpec((1, H_FOX, t_len), lambda b, h, i: (b, 0, 0)),
            pl.BlockSpec((1, LANES), lambda b, h, i: (0, h)),
        ],
        out_specs=pl.BlockSpec((1, tq, LANES), lambda b, h, i: (b, i, h)),
        out_shape=jax.ShapeDtypeStruct((bn, t_len, d), BF16),
        scratch_shapes=[pltpu.VMEM((2, t_len // tq, tq, tq), F32),
                        pltpu.VMEM((2, tq, tq), BF16),
                        pltpu.VMEM((2, tq, LANES), F32),
                        pltpu.VMEM((2, tq, LANES), F32),
                        pltpu.VMEM((2, tq, LANES), F32),
                        pltpu.VMEM((2, tq, LANES), F32)],
        compiler_params=_params("parallel", "parallel", "arbitrary"),
        name="fox_prompt",
    )(q, kt_stack, v, cum, cumt, gain.reshape(1, d))


def _suffix_kernel(lf_ref, tab_ref):
    tp, nh, n = lf_ref.shape
    i = lax.broadcasted_iota(jnp.int32, (n, n), 0)
    j = lax.broadcasted_iota(jnp.int32, (n, n), 1)
    lf = lf_ref[...].reshape(tp * nh, n)
    suf = _dot01_right(lf, (i > j).astype(BF16))
    tot = _dot01_right(lf, jnp.ones((n, n), BF16))
    tab_ref[:, :nh, :] = suf.reshape(tp, nh, n)
    tab_ref[:, nh:, :] = tot.reshape(tp, nh, n)


def _suffix(lf_t, tp=512):
    pages, nh, n = lf_t.shape
    tp = min(tp, pages)
    assert pages % tp == 0
    return pl.pallas_call(
        _suffix_kernel,
        grid=(pages // tp,),
        in_specs=[pl.BlockSpec((tp, nh, n), lambda i: (i, 0, 0))],
        out_specs=pl.BlockSpec((tp, 2 * nh, n), lambda i: (i, 0, 0)),
        out_shape=jax.ShapeDtypeStruct((pages, 2 * nh, n), F32),
        compiler_params=_params("parallel"),
        name="suffix",
    )(lf_t)


def _fox_decode_kernel(pt_ref, q_ref, kn_ref, vn_ref, cq_ref, gain_ref, tab_ref, ck_hbm, cv_hbm,
                       o_ref, m_ref, l_ref, acc_ref, carry_ref, qbd_ref, kbuf, vbuf, sem,
                       *, pps, scale, first_page):
    steps = pl.num_programs(1)
    n_pages = steps * pps
    b = pl.program_id(0)
    j = pl.program_id(1)
    t_new = q_ref.shape[1]
    d = q_ref.shape[2]
    dh = d // H_FOX
    rows = t_new * H_FOX
    page = kbuf.shape[3]

    g = b * steps + j
    total = pl.num_programs(0) * steps

    def page_copies(gg, slot):
        bb = gg // steps
        jj = gg - bb * steps
        copies = []
        for i in range(pps):
            pid = first_page + pt_ref[bb, n_pages - 1 - (jj * pps + i)]
            copies.append(pltpu.make_async_copy(ck_hbm.at[pid], kbuf.at[slot, i], sem.at[slot, 0, i]))
            copies.append(pltpu.make_async_copy(cv_hbm.at[pid], vbuf.at[slot, i], sem.at[slot, 1, i]))
        return copies

    @pl.when(g == 0)
    def _():
        for ahead in range(DECODE_SLOTS - 1):
            @pl.when(ahead < total)
            def _():
                for c in page_copies(ahead, ahead):
                    c.start()

    @pl.when(g + DECODE_SLOTS - 1 < total)
    def _():
        nxt = g + DECODE_SLOTS - 1
        for c in page_copies(nxt, nxt % DECODE_SLOTS):
            c.start()

    slot = g % DECODE_SLOTS
    for c in page_copies(g, slot):
        c.wait()
    k_refs = [kbuf.at[slot, i] for i in range(pps)]
    v_refs = [vbuf.at[slot, i] for i in range(pps)]
    head_of_lane = lax.broadcasted_iota(jnp.int32, (H_FOX, d), 1) // dh
    head_of_row = lax.broadcasted_iota(jnp.int32, (H_FOX, d), 0)
    diag = head_of_lane == head_of_row
    cq = cq_ref[0]

    @pl.when(j == 0)
    def _():
        qbd = jnp.concatenate(
            [jnp.where(diag, q_ref[0, t:t + 1, :] * scale, 0.0) for t in range(t_new)], axis=0)
        qbd_ref[...] = qbd
        qrow = lax.broadcasted_iota(jnp.int32, (rows, 1), 0) // H_FOX
        s_new = []
        for s in range(t_new):
            dot_s = jnp.sum(qbd * kn_ref[0, s:s + 1, :], axis=-1, keepdims=True)
            cs = jnp.concatenate([cq[s * H_FOX:(s + 1) * H_FOX]] * t_new, axis=0)
            s_new.append(jnp.where(qrow >= s, dot_s + cq - cs, NEG))
        m0 = s_new[0]
        for s in range(1, t_new):
            m0 = jnp.maximum(m0, s_new[s])
        l0 = jnp.zeros((rows, 1), F32)
        acc0 = jnp.zeros((rows, d), F32)
        for s in range(t_new):
            p_s = jnp.exp(s_new[s] - m0)
            l0 = l0 + p_s
            acc0 = acc0 + p_s * vn_ref[0, s:s + 1, :]
        m_ref[...] = jnp.broadcast_to(m0, (rows, LANES))
        l_ref[...] = jnp.broadcast_to(l0, (rows, LANES))
        acc_ref[...] = acc0
        carry_ref[...] = jnp.zeros_like(carry_ref)

    qbd = qbd_ref[...].astype(BF16)
    carry = carry_ref[...]
    groups = [range(g0, min(g0 + DECODE_GROUP, pps)) for g0 in range(0, pps, DECODE_GROUP)]
    scores = []
    for group in groups:
        s_parts = []
        for i in group:
            kpg = k_refs[i][...].astype(BF16)
            s = jnp.dot(qbd, kpg, preferred_element_type=F32)
            pid = pt_ref[b, n_pages - 1 - (j * pps + i)]
            suf = tab_ref[pid, :H_FOX, :] + carry
            carry = carry + tab_ref[pid, H_FOX:, :]
            s_parts.append(s + jnp.concatenate([suf] * t_new, axis=0) + cq)
        scores.append(jnp.concatenate(s_parts, axis=1))
    carry_ref[...] = carry
    softmaxes = []
    for s_g in scores:
        m_g = jnp.max(s_g, axis=-1, keepdims=True)
        p = jnp.exp(s_g - m_g)
        softmaxes.append((m_g, jnp.sum(p, axis=-1, keepdims=True), p.astype(BF16)))
    partials = []
    for group, (m_g, l_g, p_bf) in zip(groups, softmaxes):
        vt = jnp.concatenate([v_refs[i][...] for i in group], axis=1).astype(BF16)
        partials.append((m_g, l_g, lax.dot_general(p_bf, vt, NT_DIMS, preferred_element_type=F32)))
    m_prev = m_ref[...][:, :1]
    m_new = m_prev
    for m_g, _, _ in partials:
        m_new = jnp.maximum(m_new, m_g)
    alpha = jnp.exp(m_prev - m_new)
    l_new = alpha * l_ref[...][:, :1]
    acc = alpha * acc_ref[...]
    for m_g, l_g, acc_g in partials:
        w_g = jnp.exp(m_g - m_new)
        l_new = l_new + w_g * l_g
        acc = acc + w_g * acc_g
    m_ref[...] = jnp.broadcast_to(m_new, m_ref.shape)
    l_ref[...] = jnp.broadcast_to(l_new, l_ref.shape)
    acc_ref[...] = acc

    @pl.when(j == pl.num_programs(1) - 1)
    def _():
        o = acc_ref[...] / l_ref[...][:, :1]
        o = jnp.where(jnp.concatenate([diag] * t_new, axis=0), o, 0.0)
        y = o * lax.rsqrt(jnp.sum(o * o, axis=-1, keepdims=True) / dh + EPS)
        out = jnp.concatenate(
            [jnp.sum(y[t * H_FOX:(t + 1) * H_FOX], axis=0, keepdims=True) for t in range(t_new)],
            axis=0)
        o_ref[0] = out * gain_ref[...]


def _fox_decode(q, k_new, v_new, cq_col, gain, cache_k, cache_v, suffix_tab, page_table,
                layer, n_pool, pps=16):
    bn, t_new, d = q.shape
    page = cache_k.shape[2]
    n_pages = page_table.shape[1]
    pps = min(pps, n_pages)
    assert n_pages % pps == 0
    rows = t_new * H_FOX
    scale = float((d // H_FOX) ** -0.5)
    tok = pl.BlockSpec((1, t_new, d), lambda b, j, pt: (b, 0, 0))
    in_specs = [tok, tok, tok,
                pl.BlockSpec((1, rows, 1), lambda b, j, pt: (b, 0, 0)),
                pl.BlockSpec((1, d), lambda b, j, pt: (0, 0)),
                _resident((None, n_pool, 2 * H_FOX, page), lambda b, j, pt: (layer, 0, 0, 0)),
                pl.BlockSpec(memory_space=pl.ANY),
                pl.BlockSpec(memory_space=pl.ANY)]
    grid_spec = pltpu.PrefetchScalarGridSpec(
        num_scalar_prefetch=1,
        grid=(bn, n_pages // pps),
        in_specs=in_specs,
        out_specs=pl.BlockSpec((1, t_new, d), lambda b, j, pt: (b, 0, 0)),
        scratch_shapes=[pltpu.VMEM((rows, LANES), F32),
                        pltpu.VMEM((rows, LANES), F32),
                        pltpu.VMEM((rows, d), F32),
                        pltpu.VMEM((H_FOX, page), F32),
                        pltpu.VMEM((rows, d), F32),
                        pltpu.VMEM((DECODE_SLOTS, pps, d, page), F32),
                        pltpu.VMEM((DECODE_SLOTS, pps, d, page), F32),
                        pltpu.SemaphoreType.DMA((DECODE_SLOTS, 2, pps))],
    )
    return pl.pallas_call(
        functools.partial(_fox_decode_kernel, pps=pps, scale=scale, first_page=layer * n_pool),
        grid_spec=grid_spec,
        out_shape=jax.ShapeDtypeStruct((bn, t_new, d), F32),
        compiler_params=_params("arbitrary", "arbitrary"),
        name="fox_decode",
    )(page_table, q, k_new, v_new, cq_col, gain.reshape(1, d), suffix_tab, cache_k, cache_v)


def kernel(x_prompt, x_sample, cache_k, cache_v, cache_logf, state_hgrn, page_table, norm_ffn1, ffn1_w_in, ffn1_w_out, norm_mix, w_in_mix, hgrn_lb, fox_f_bias, hgrn_out_gain, fox_out_gain, w_out_mix, norm_ffn2, ffn2_w_in, ffn2_w_out, norm_final):
    depth = norm_ffn1.shape[0]
    bn, t_len, d = x_prompt.shape
    db, t_new, _ = x_sample.shape
    d_fox = fox_out_gain.shape[1]
    d_h = hgrn_out_gain.shape[1]
    dh_fox = d_fox // H_FOX
    n_pool, page = cache_k.shape[1], cache_k.shape[2]
    d_main = 4 * d_h + 3 * d_fox

    w1_in, w1_out = ffn1_w_in.astype(BF16), ffn1_w_out.astype(BF16)
    w2_in, w2_out = ffn2_w_in.astype(BF16), ffn2_w_out.astype(BF16)
    w_mix = w_in_mix.astype(BF16)
    w_f = jnp.pad(w_in_mix[:, :, d_main:], ((0, 0), (0, 0), (0, LANES - H_FOX))).astype(BF16)
    f_bias = jnp.pad(fox_f_bias, ((0, 0), (0, LANES - H_FOX))).reshape(depth, 1, LANES)
    w_out = w_out_mix.astype(BF16)

    ck = jnp.transpose(cache_k, (0, 1, 3, 4, 2)).reshape(depth * n_pool, d_fox, page)
    cv = jnp.transpose(cache_v, (0, 1, 3, 4, 2)).reshape(depth * n_pool, d_fox, page)
    lf_t = jnp.swapaxes(cache_logf, 2, 3).reshape(depth * n_pool, H_FOX, page)
    suffix_tab = _suffix(lf_t).reshape(depth, n_pool, 2 * H_FOX, page)

    bf16_rows = 2 * SUBLANES
    t_pad = -(-t_new // bf16_rows) * bf16_rows

    def run(x, seq_len, nseq, prompt):
        ks, vs, lfs, ss = [], [], [], []
        kv_stack = None
        for l in range(depth):
            x = _ffn(x, norm_ffn1[l], w1_in, w1_out, l)
            if prompt:
                hg, q, v, lf, cum, cumt, *kv_stack = _inproj(
                    x, norm_mix[l], w_mix, w_f, f_bias, l, seq_len, 4 * d_h, d_fox,
                    earlier=kv_stack, tm=min(512, seq_len))
                o_a, s_fin = _hgrn(hg.reshape(nseq, seq_len, 4 * d_h), hgrn_lb, hgrn_out_gain[l], None, l,
                                   tb=min(2048, seq_len), chunk=64, sub=16, valid_len=None, out_dtype=BF16)
                o_a = o_a.reshape(nseq * seq_len, d_h)
                tiles = cumt.shape[0] // nseq
                cumt_b = cumt.reshape(nseq, tiles, H_FOX, -1).transpose(0, 2, 1, 3).reshape(nseq, H_FOX, seq_len)
                o_f = _fox_prompt(q.reshape(nseq, seq_len, d_fox), kv_stack[0], l,
                                  v.reshape(nseq, seq_len, d_fox), cum.reshape(nseq, seq_len, H_FOX),
                                  cumt_b, fox_out_gain[l]).reshape(nseq * seq_len, d_fox)
            else:
                hg, q, k, v, lf, cum, cumt = _inproj(x, norm_mix[l], w_mix, w_f, f_bias, l, seq_len,
                                                     4 * d_h, d_fox)
                ks.append(k.reshape(nseq, seq_len, H_FOX, dh_fox))
                vs.append(v.reshape(nseq, seq_len, H_FOX, dh_fox))
                hg_p = jnp.pad(hg.reshape(nseq, seq_len, 4 * d_h), ((0, 0), (0, t_pad - seq_len), (0, 0)))
                o_a, s_fin = _hgrn(hg_p, hgrn_lb, hgrn_out_gain[l], state_hgrn, l,
                                   tb=t_pad, chunk=t_pad, sub=t_pad, valid_len=seq_len, out_dtype=F32)
                o_a = o_a[:, :seq_len].reshape(nseq * seq_len, d_h).astype(BF16)
                o_f = _fox_decode(q.reshape(nseq, seq_len, d_fox), k.reshape(nseq, seq_len, d_fox),
                                  v.reshape(nseq, seq_len, d_fox), cum.reshape(nseq, seq_len * H_FOX, 1),
                                  fox_out_gain[l], ck, cv, suffix_tab, page_table, l, n_pool)
                o_f = o_f.reshape(nseq * seq_len, d_fox).astype(BF16)
            x = _ffn(x, norm_ffn2[l], w2_in, w2_out, l, mixer=(o_a, o_f, w_out),
                     final_g=norm_final if l == depth - 1 else None)
            lfs.append(lf.reshape(nseq, seq_len, H_FOX))
            ss.append(s_fin)
        if prompt:
            k_all, v_all = (a.reshape(depth, nseq, H_FOX, dh_fox, seq_len).transpose(0, 1, 4, 2, 3)
                            for a in kv_stack)
        else:
            k_all, v_all = jnp.stack(ks), jnp.stack(vs)
        return (x.reshape(nseq, seq_len, d), k_all, v_all, jnp.stack(lfs), jnp.stack(ss))

    y_p, k_p, v_p, lf_p, s_p = run(x_prompt.reshape(bn * t_len, d), t_len, bn, True)
    y_s, k_s, v_s, lf_s, s_s = run(x_sample.reshape(db * t_new, d), t_new, db, False)
    return (y_p, y_s, k_p, v_p, lf_p, s_p, k_s, v_s, lf_s, s_s)
```
